```python
import jax, jax.numpy as jnp
from jax import lax
import numpy as np

D_MODEL = 2048
BATCH = 8
SEQ = 2048
DEPTH = 4

CHUNK = 64
GMLP_BLOCK = 128
D_A = D_MODEL // 2
N_GROUPS_A = 4
D_B = D_MODEL // 2
CONV_B = 31
D_C = D_MODEL // 2
POOL_WINDOWS = (2, 4, 8, 16)
D_D = D_MODEL // 2
CONV_D = 3
D_FF = ((8 * D_MODEL // 3 + 255) // 256) * 256
N_EXPERTS = 8
TOP_K = 2
D_EXPERT = 7 * D_MODEL // 2
MOE_BLOCK = 128
N_EVEN = (DEPTH + 1) // 2
N_ODD = DEPTH // 2
EPS = 1e-6

kernel_name = 'hybrid_streaming_encoder_adaln_moe'


def rms_norm(x, g):
    xf = x.astype(jnp.float32)
    y = xf * lax.rsqrt(jnp.mean(xf * xf, axis=-1, keepdims=True) + EPS)
    return (y * g.astype(jnp.float32)).astype(x.dtype)


def layer_norm(x, g, b):
    xf = x.astype(jnp.float32)
    mu = jnp.mean(xf, axis=-1, keepdims=True)
    xc = xf - mu
    y = xc * lax.rsqrt(jnp.mean(xc * xc, axis=-1, keepdims=True) + EPS)
    return (y * g.astype(jnp.float32) + b.astype(jnp.float32)).astype(x.dtype)


def causal_dwconv(x, w, b):
    k = w.shape[0]
    y = lax.conv_general_dilated(
        x, w[:, None, :].astype(x.dtype), window_strides=(1,), padding=[(k - 1, 0)],
        dimension_numbers=('NWC', 'WIO', 'NWC'), feature_group_count=x.shape[-1])
    return y + b.astype(x.dtype)


def block_causal_mask(dtype):
    pos = np.arange(GMLP_BLOCK) // CHUNK
    return jnp.asarray(pos[None, :] <= pos[:, None], dtype=dtype)


def even_mixer(h, w_in, w_out, gm_ws, gm_b, gm_norm_g, cv_w, cv_b, cv_ln_g, cv_ln_b):
    bsz, s, _ = h.shape
    z = h @ w_in
    u, v, a, gate = jnp.split(z, [D_A, 2 * D_A, 2 * D_A + D_B], axis=-1)
    u = jax.nn.gelu(u)
    v = rms_norm(jax.nn.gelu(v), gm_norm_g)
    dg = D_A // N_GROUPS_A
    nb = s // GMLP_BLOCK
    v = v.reshape(bsz, nb, GMLP_BLOCK, N_GROUPS_A, dg)
    ws = gm_ws * block_causal_mask(gm_ws.dtype)[None]
    sp = jnp.einsum('gij,bnjgc->bnigc', ws, v) + gm_b.T[None, None, :, :, None]
    y_a = u * sp.reshape(bsz, s, D_A)
    glu = a * jax.nn.sigmoid(gate)
    y_b = jax.nn.silu(layer_norm(causal_dwconv(glu, cv_w, cv_b), cv_ln_g, cv_ln_b))
    return jnp.concatenate([y_a, y_b], axis=-1) @ w_out


def multiscale_pool(hc, pool_w, pool_scale):
    bsz, s, _ = hc.shape
    ng = len(POOL_WINDOWS)
    dg = D_C // ng
    hf = hc.astype(jnp.float32)
    cs = jnp.pad(jnp.cumsum(hf, axis=1), ((0, 0), (1, 0), (0, 0)))
    t = jnp.arange(s)
    outs = []
    for g, w in enumerate(POOL_WINDOWS):
        seg = cs[:, :, g * dg:(g + 1) * dg]
        lo = jnp.take(seg, jnp.maximum(t + 1 - w, 0), axis=1)
        count = jnp.minimum(t + 1, w).astype(jnp.float32)
        outs.append((seg[:, 1:] - lo) / count[None, :, None])
    pooled = jnp.concatenate(outs, axis=-1)
    diff = (pooled - hf).astype(hc.dtype).reshape(bsz, s, ng, dg)
    mixed = jnp.einsum('bsgc,gcd->bsgd', diff, pool_w).reshape(bsz, s, D_C)
    return mixed * pool_scale


def odd_mixer(h, w_in, w_out, pool_w, pool_scale, sc_w, sc_b):
    z = h @ w_in
    hc, bg, cg, hd = jnp.split(z, [D_C, D_C + D_D, D_C + 2 * D_D], axis=-1)
    y_c = multiscale_pool(hc, pool_w, pool_scale)
    y_d = bg * causal_dwconv(cg * hd, sc_w, sc_b)
    return jnp.concatenate([y_c, y_d], axis=-1) @ w_out


def dense_swiglu(h, w1, w3, w2):
    return (jax.nn.silu(h @ w1) * (h @ w3)) @ w2


def moe_swiglu(h, router_w, w1, w3, w2):
    bsz, s, d = h.shape
    xf = h.reshape(-1, d)
    n = xf.shape[0]
    logits = (xf @ router_w).astype(jnp.float32)
    top_val, top_idx = lax.top_k(logits, TOP_K)
    gates = jax.nn.softmax(top_val, axis=-1)
    e_flat = top_idx.reshape(-1)
    t_flat = jnp.repeat(jnp.arange(n, dtype=jnp.int32), TOP_K)
    g_flat = gates.reshape(-1)
    order = jnp.argsort(e_flat)
    e_s, t_s, g_s = e_flat[order], t_flat[order], g_flat[order]
    counts = jnp.bincount(e_flat, length=N_EXPERTS)
    padded = (counts + MOE_BLOCK - 1) // MOE_BLOCK * MOE_BLOCK
    start = jnp.cumsum(counts) - counts
    pend = jnp.cumsum(padded)
    pstart = pend - padded
    dest = pstart[e_s] + jnp.arange(TOP_K * n) - start[e_s]
    n_blocks = (TOP_K * n + MOE_BLOCK - 1) // MOE_BLOCK + N_EXPERTS
    rows = n_blocks * MOE_BLOCK
    tok_buf = jnp.zeros((rows,), jnp.int32).at[dest].set(t_s)
    gate_buf = jnp.zeros((rows,), jnp.float32).at[dest].set(g_s)
    blk_start = jnp.arange(n_blocks) * MOE_BLOCK
    blk_expert = jnp.minimum(jnp.sum(blk_start[:, None] >= pend[None, :], axis=1), N_EXPERTS - 1)

    def run_block(args):
        tok, e = args
        xb = xf[tok]
        hb = jax.nn.silu(xb @ w1[e]) * (xb @ w3[e])
        return hb @ w2[e]

    yb = lax.map(run_block, (tok_buf.reshape(n_blocks, MOE_BLOCK), blk_expert))
    contrib = yb.reshape(rows, d) * gate_buf[:, None].astype(xf.dtype)
    y = jnp.zeros_like(xf).at[tok_buf].add(contrib)
    return y.reshape(bsz, s, d)


def setup_inputs(seed: int = 0) -> dict:
    key = jax.random.key(seed)
    ks = iter(jax.random.split(key, 40))

    def nrm(shape, scale):
        return jax.random.normal(next(ks), shape, jnp.float32) * scale

    def gain(shape):
        return 1.0 + nrm(shape, 0.05)

    d = D_MODEL
    dgc = D_C // len(POOL_WINDOWS)
    return {
        'x': nrm((BATCH, SEQ, d), 1.0),
        'c': nrm((BATCH, d), 1.0),
        'ada_w': nrm((d, 6 * d), d ** -0.5),
        'ada_b': nrm((DEPTH, 6 * d), 0.02),
        'norm_mix_g': gain((DEPTH, d)),
        'norm_ffn_g': gain((DEPTH, d)),
        'final_g': gain((d,)),
        'ev_w_in': nrm((N_EVEN, d, 2 * D_A + 2 * D_B), d ** -0.5),
        'ev_w_out': nrm((N_EVEN, D_A + D_B, d), (D_A + D_B) ** -0.5),
        'gm_ws': nrm((N_EVEN, N_GROUPS_A, GMLP_BLOCK, GMLP_BLOCK), 0.5 * GMLP_BLOCK ** -0.5),
        'gm_b': 1.0 + nrm((N_EVEN, N_GROUPS_A, GMLP_BLOCK), 0.1),
        'gm_norm_g': gain((N_EVEN, D_A)),
        'cv_w': nrm((N_EVEN, CONV_B, D_B), CONV_B ** -0.5),
        'cv_b': nrm((N_EVEN, D_B), 0.02),
        'cv_ln_g': gain((N_EVEN, D_B)),
        'cv_ln_b': nrm((N_EVEN, D_B), 0.02),
        'ffn_w1': nrm((N_EVEN, d, D_FF), d ** -0.5),
        'ffn_w3': nrm((N_EVEN, d, D_FF), d ** -0.5),
        'ffn_w2': nrm((N_EVEN, D_FF, d), D_FF ** -0.5),
        'od_w_in': nrm((N_ODD, d, D_C + 3 * D_D), d ** -0.5),
        'od_w_out': nrm((N_ODD, D_C + D_D, d), (D_C + D_D) ** -0.5),
        'pool_w': nrm((N_ODD, len(POOL_WINDOWS), dgc, dgc), dgc ** -0.5),
        'pool_scale': gain((N_ODD, D_C)),
        'sc_w': nrm((N_ODD, CONV_D, D_D), CONV_D ** -0.5),
        'sc_b': nrm((N_ODD, D_D), 0.02),
        'router_w': nrm((N_ODD, d, N_EXPERTS), d ** -0.5),
        'moe_w1': nrm((N_ODD, N_EXPERTS, d, D_EXPERT), d ** -0.5),
        'moe_w3': nrm((N_ODD, N_EXPERTS, d, D_EXPERT), d ** -0.5),
        'moe_w2': nrm((N_ODD, N_EXPERTS, D_EXPERT, d), D_EXPERT ** -0.5),
    }


def reference(x, c, ada_w, ada_b, norm_mix_g, norm_ffn_g, final_g,
              ev_w_in, ev_w_out, gm_ws, gm_b, gm_norm_g, cv_w, cv_b, cv_ln_g, cv_ln_b,
              ffn_w1, ffn_w3, ffn_w2,
              od_w_in, od_w_out, pool_w, pool_scale, sc_w, sc_b,
              router_w, moe_w1, moe_w3, moe_w2):
    base_mod = jax.nn.silu(c) @ ada_w
    for l in range(DEPTH):
        mod = (base_mod + ada_b[l])[:, None, :]
        sh1, sc1, g1, sh2, sc2, g2 = jnp.split(mod, 6, axis=-1)
        h = rms_norm(x, norm_mix_g[l]) * (1.0 + sc1) + sh1
        if l % 2 == 0:
            i = l // 2
            y = even_mixer(h, ev_w_in[i], ev_w_out[i], gm_ws[i], gm_b[i], gm_norm_g[i],
                           cv_w[i], cv_b[i], cv_ln_g[i], cv_ln_b[i])
        else:
            i = l // 2
            y = odd_mixer(h, od_w_in[i], od_w_out[i], pool_w[i], pool_scale[i], sc_w[i], sc_b[i])
        x = x + g1 * y
        h = rms_norm(x, norm_ffn_g[l]) * (1.0 + sc2) + sh2
        if l % 2 == 0:
            i = l // 2
            y = dense_swiglu(h, ffn_w1[i], ffn_w3[i], ffn_w2[i])
        else:
            i = l // 2
            y = moe_swiglu(h, router_w[i], moe_w1[i], moe_w3[i], moe_w2[i])
        x = x + g2 * y
    return rms_norm(x, final_g)
```

```python
import functools

import jax
import jax.numpy as jnp
from jax import lax
from jax.experimental import pallas as pl
from jax.experimental.pallas import tpu as pltpu

BF = jnp.bfloat16
F32 = jnp.float32
EPS = 1e-6

CHUNK = 64
GMLP_BLOCK = 128
N_GROUPS_A = 4
CONV_B = 31
CONV_D = 3
POOL_WINDOWS = (2, 4, 8, 16)
N_EXPERTS = 8
TOP_K = 2

LANES = 128
HALO = 32
VMEM_LIMIT = 56 * 1024 * 1024

TM_MIX = 256
TM_IN = 512
TM_MOE = 512
TM_GATHER = 256


def _params(sem):
    return pltpu.CompilerParams(dimension_semantics=sem, vmem_limit_bytes=VMEM_LIMIT)


def _resident(shape, index_map):
    return pl.BlockSpec(shape, index_map, pipeline_mode=pl.Buffered(1))


def _tile(n, want):
    t = min(want, n)
    while n % t:
        t -= LANES
    return t


def _rms(x, g):
    return x * lax.rsqrt(jnp.mean(x * x, axis=-1, keepdims=True) + EPS) * g


def _ada_kernel(c_ref, w_ref, b_ref, o_ref):
    c = c_ref[...]
    s = c * jax.nn.sigmoid(c)
    base = jnp.dot(s, w_ref[...], preferred_element_type=F32, precision=lax.Precision.HIGHEST)
    o_ref[...] = base[None, :, :] + b_ref[...][:, None, :]


def _ada(c, ada_w, ada_b, tn=1024):
    bsz, d = c.shape
    depth, n6 = ada_b.shape
    tn = min(tn, n6)
    return pl.pallas_call(
        _ada_kernel,
        grid=(n6 // tn,),
        in_specs=[pl.BlockSpec((bsz, d), lambda j: (0, 0)),
                  pl.BlockSpec((d, tn), lambda j: (0, j)),
                  pl.BlockSpec((depth, tn), lambda j: (0, j))],
        out_specs=pl.BlockSpec((depth, bsz, tn), lambda j: (0, 0, j)),
        out_shape=jax.ShapeDtypeStruct((depth, bsz, n6), F32),
        compiler_params=_params(("arbitrary",)),
        name="ada_mod",
    )(c, ada_w, ada_b)


def _norm_mm_kernel(x_ref, g_ref, sc_ref, sh_ref, w_ref, z_ref, *, tn):
    h = _rms(x_ref[...], g_ref[...]) * (1.0 + sc_ref[0]) + sh_ref[0]
    hb = h.astype(BF)
    nout = z_ref.shape[1]
    for n0 in range(0, nout, tn):
        z_ref[:, n0:n0 + tn] = jnp.dot(hb, w_ref[:, n0:n0 + tn], preferred_element_type=F32)


def _norm_mm(x, g, sc, sh, w, seq, tm=TM_IN, tn=512):
    n, d = x.shape
    nout = w.shape[1]
    tm = min(tm, seq)
    bpb = seq // tm
    return pl.pallas_call(
        functools.partial(_norm_mm_kernel, tn=min(tn, nout)),
        grid=(n // tm,),
        in_specs=[pl.BlockSpec((tm, d), lambda i: (i, 0)),
                  _resident((1, d), lambda i: (0, 0)),
                  pl.BlockSpec((1, 1, d), lambda i: (i // bpb, 0, 0)),
                  pl.BlockSpec((1, 1, d), lambda i: (i // bpb, 0, 0)),
                  _resident((d, nout), lambda i: (0, 0))],
        out_specs=pl.BlockSpec((tm, nout), lambda i: (i, 0)),
        out_shape=jax.ShapeDtypeStruct((n, nout), F32),
        compiler_params=_params(("arbitrary",)),
        name="norm_in_proj",
    )(x, g.reshape(1, d), sc, sh, w)


def _out_proj_residual(ycat, x_ref, g1_ref, wout_ref, xo_ref, tn=512):
    d = xo_ref.shape[1]
    yc = ycat[...]
    for n0 in range(0, d, tn):
        y = jnp.dot(yc, wout_ref[:, n0:n0 + tn], preferred_element_type=F32)
        xo_ref[:, n0:n0 + tn] = x_ref[:, n0:n0 + tn] + g1_ref[0][:, n0:n0 + tn] * y


def _ffn_input(xo_ref, fg_ref, sc2_ref, sh2_ref):
    return _rms(xo_ref[...], fg_ref[...]) * (1.0 + sc2_ref[0]) + sh2_ref[0]


def _even_mix_kernel(z_ref, halo_ref, x_ref, g1_ref, wout_ref, ws_ref, gmb_ref, gng_ref,
                     cvw_ref, cvb_ref, lng_ref, lnb_ref, fg_ref, sc2_ref, sh2_ref,
                     xo_ref, h2_ref, gbuf, conv, ycat, *, tm, da, db, bpb):
    i = pl.program_id(0)
    dg = da // N_GROUPS_A

    r = lax.broadcasted_iota(jnp.int32, (GMLP_BLOCK, GMLP_BLOCK), 0) // CHUNK
    c = lax.broadcasted_iota(jnp.int32, (GMLP_BLOCK, GMLP_BLOCK), 1) // CHUNK
    causal = c <= r
    ws = [jnp.where(causal, ws_ref[g], 0.0).astype(BF) for g in range(N_GROUPS_A)]
    for n in range(tm // GMLP_BLOCK):
        rows = slice(n * GMLP_BLOCK, (n + 1) * GMLP_BLOCK)
        v = jax.nn.gelu(z_ref[rows, da:2 * da], approximate=True)
        vb = _rms(v, gng_ref[...]).astype(BF)
        for g in range(N_GROUPS_A):
            cols = slice(g * dg, (g + 1) * dg)
            sp = jnp.dot(ws[g], vb[:, cols], preferred_element_type=F32) + gmb_ref[:, g:g + 1]
            u = jax.nn.gelu(z_ref[rows, cols], approximate=True)
            ycat[rows, cols] = (u * sp).astype(BF)

    a = z_ref[:, 2 * da:2 * da + db]
    gate = z_ref[:, 2 * da + db:]
    gbuf[HALO:HALO + tm, :] = a * jax.nn.sigmoid(gate)
    hglu = halo_ref[:, 0:db] * jax.nn.sigmoid(halo_ref[:, db:2 * db])
    gbuf[0:HALO, :] = jnp.where(i % bpb == 0, 0.0, hglu)

    rc = min(64, tm)
    off = HALO - (CONV_B - 1)

    def conv_lanes(cb, carry):
        lanes = pl.ds(pl.multiple_of(cb * LANES, LANES), LANES)
        wk = [cvw_ref[k:k + 1, lanes] for k in range(CONV_B)]
        bias = cvb_ref[:, lanes]
        for r0 in range(0, tm, rc):
            acc = wk[0] * gbuf[pl.ds(off + r0, rc), lanes]
            for k in range(1, CONV_B):
                acc = acc + wk[k] * gbuf[pl.ds(off + k + r0, rc), lanes]
            conv[pl.ds(r0, rc), lanes] = acc + bias
        return carry

    lax.fori_loop(0, db // LANES, conv_lanes, 0)

    cv = conv[...]
    mu = jnp.mean(cv, axis=-1, keepdims=True)
    xc = cv - mu
    ln = xc * lax.rsqrt(jnp.mean(xc * xc, axis=-1, keepdims=True) + EPS) * lng_ref[...] + lnb_ref[...]
    ycat[:, da:da + db] = (ln * jax.nn.sigmoid(ln)).astype(BF)

    _out_proj_residual(ycat, x_ref, g1_ref, wout_ref, xo_ref)
    h2_ref[...] = _ffn_input(xo_ref, fg_ref, sc2_ref, sh2_ref).astype(h2_ref.dtype)


def _even_mix(z, x, g1, wout, gm_ws, gm_b, gm_norm_g, cv_w, cv_b, cv_ln_g, cv_ln_b,
              ffn_g, sc2, sh2, seq, tm=TM_MIX):
    n, d = x.shape
    da = gm_norm_g.shape[0]
    db = cv_b.shape[0]
    tm = min(tm, seq)
    bpb = seq // tm
    hb = tm // HALO
    vec = lambda k: _resident((1, k), lambda i: (0, 0))
    per_batch = pl.BlockSpec((1, 1, d), lambda i: (i // bpb, 0, 0))
    return pl.pallas_call(
        functools.partial(_even_mix_kernel, tm=tm, da=da, db=db, bpb=bpb),
        grid=(n // tm,),
        in_specs=[pl.BlockSpec((tm, 2 * da + 2 * db), lambda i: (i, 0)),
                  pl.BlockSpec((HALO, 2 * db), lambda i: (jnp.maximum(i * hb - 1, 0), da // db)),
                  pl.BlockSpec((tm, d), lambda i: (i, 0)),
                  per_batch,
                  _resident((da + db, d), lambda i: (0, 0)),
                  _resident((N_GROUPS_A, GMLP_BLOCK, GMLP_BLOCK), lambda i: (0, 0, 0)),
                  _resident((GMLP_BLOCK, N_GROUPS_A), lambda i: (0, 0)),
                  vec(da),
                  _resident((CONV_B, db), lambda i: (0, 0)),
                  vec(db), vec(db), vec(db), vec(d),
                  per_batch, per_batch],
        out_specs=[pl.BlockSpec((tm, d), lambda i: (i, 0)),
                   pl.BlockSpec((tm, d), lambda i: (i, 0))],
        out_shape=[jax.ShapeDtypeStruct((n, d), F32),
                   jax.ShapeDtypeStruct((n, d), BF)],
        scratch_shapes=[pltpu.VMEM((HALO + tm, db), F32),
                        pltpu.VMEM((tm, db), F32),
                        pltpu.VMEM((tm, da + db), BF)],
        compiler_params=_params(("arbitrary",)),
        name="even_mixer_out_proj",
    )(z, z, x, g1, wout, gm_ws, gm_b.T, gm_norm_g.reshape(1, da), cv_w, cv_b.reshape(1, db),
      cv_ln_g.reshape(1, db), cv_ln_b.reshape(1, db), ffn_g.reshape(1, d), sc2, sh2)


def _odd_mix_kernel(z_ref, halo_ref, x_ref, g1_ref, wout_ref, pw_ref, ps_ref, scw_ref, scb_ref,
                    fg_ref, sc2_ref, sh2_ref, rw_ref,
                    xo_ref, h2_ref, ti_ref, tg_ref, pbuf, qbuf, ycat, *, tm, dc, dd, bpb):
    i = pl.program_id(0)
    first = i % bpb == 0
    dgc = dc // len(POOL_WINDOWS)

    pbuf[HALO:HALO + tm, :] = z_ref[:, 0:dc]
    pbuf[0:HALO, :] = jnp.where(first, 0.0, halo_ref[:, 0:dc])
    pos = (i % bpb) * tm + lax.broadcasted_iota(jnp.int32, (tm, 1), 0)
    for g, w in enumerate(POOL_WINDOWS):
        cols = slice(g * dgc, (g + 1) * dgc)
        tok = pbuf[HALO:HALO + tm, cols]
        acc = tok
        for dlt in range(1, w):
            acc = acc + pbuf[HALO - dlt:HALO - dlt + tm, cols]
        cnt = jnp.minimum(pos + 1, w).astype(F32)
        diff = (acc / cnt - tok).astype(BF)
        mixed = jnp.dot(diff, pw_ref[g], preferred_element_type=F32) * ps_ref[:, cols]
        ycat[:, cols] = mixed.astype(BF)

    qbuf[HALO:HALO + tm, :] = z_ref[:, dc + dd:dc + 2 * dd] * z_ref[:, dc + 2 * dd:]
    hp = halo_ref[:, dc + dd:dc + 2 * dd] * halo_ref[:, dc + 2 * dd:]
    qbuf[0:HALO, :] = jnp.where(first, 0.0, hp)
    off = HALO - (CONV_D - 1)
    y = scw_ref[0:1, :] * qbuf[off:off + tm, :]
    for k in range(1, CONV_D):
        y = y + scw_ref[k:k + 1, :] * qbuf[off + k:off + k + tm, :]
    y = y + scb_ref[...]
    ycat[:, dc:dc + dd] = (z_ref[:, dc:dc + dd] * y).astype(BF)

    _out_proj_residual(ycat, x_ref, g1_ref, wout_ref, xo_ref)
    h2 = _ffn_input(xo_ref, fg_ref, sc2_ref, sh2_ref)
    h2_ref[...] = h2

    logits = jnp.dot(h2, rw_ref[...], preferred_element_type=F32, precision=lax.Precision.HIGHEST)
    lane = lax.broadcasted_iota(jnp.int32, logits.shape, 1)
    lane_f = lane.astype(F32)
    neg = jnp.float32(-jnp.inf)
    l1 = jnp.where(lane < N_EXPERTS, logits, neg)
    m1 = jnp.max(l1, axis=-1, keepdims=True)
    i1 = jnp.min(jnp.where(l1 == m1, lane_f, float(LANES)), axis=-1, keepdims=True)
    l2 = jnp.where(lane_f == i1, neg, l1)
    m2 = jnp.max(l2, axis=-1, keepdims=True)
    i2 = jnp.min(jnp.where(l2 == m2, lane_f, float(LANES)), axis=-1, keepdims=True)
    e = jnp.exp(m2 - m1)
    ga = 1.0 / (1.0 + e)
    gb = e / (1.0 + e)
    ti_ref[...] = jnp.where(lane == 0, i1, jnp.where(lane == 1, i2, 0.0)).astype(jnp.int32)
    tg_ref[...] = jnp.where(lane == 0, ga, jnp.where(lane == 1, gb, 0.0))


def _odd_mix(z, x, g1, wout, pool_w, pool_scale, sc_w, sc_b, ffn_g, sc2, sh2, router_w, seq, tm=TM_MIX):
    n, d = x.shape
    dc = pool_scale.shape[0]
    dd = sc_b.shape[0]
    ng = len(POOL_WINDOWS)
    dgc = dc // ng
    tm = min(tm, seq)
    bpb = seq // tm
    hb = tm // HALO
    rw = jnp.zeros((d, LANES), F32).at[:, :N_EXPERTS].set(router_w)
    vec = lambda k: _resident((1, k), lambda i: (0, 0))
    per_batch = pl.BlockSpec((1, 1, d), lambda i: (i // bpb, 0, 0))
    return pl.pallas_call(
        functools.partial(_odd_mix_kernel, tm=tm, dc=dc, dd=dd, bpb=bpb),
        grid=(n // tm,),
        in_specs=[pl.BlockSpec((tm, dc + 3 * dd), lambda i: (i, 0)),
                  pl.BlockSpec((HALO, dc + 3 * dd), lambda i: (jnp.maximum(i * hb - 1, 0), 0)),
                  pl.BlockSpec((tm, d), lambda i: (i, 0)),
                  per_batch,
                  _resident((dc + dd, d), lambda i: (0, 0)),
                  _resident((ng, dgc, dgc), lambda i: (0, 0, 0)),
                  vec(dc),
                  _resident((CONV_D, dd), lambda i: (0, 0)),
                  vec(dd), vec(d),
                  per_batch, per_batch,
                  _resident((d, LANES), lambda i: (0, 0))],
        out_specs=[pl.BlockSpec((tm, d), lambda i: (i, 0)),
                   pl.BlockSpec((tm, d), lambda i: (i, 0)),
                   pl.BlockSpec((tm, LANES), lambda i: (i, 0)),
                   pl.BlockSpec((tm, LANES), lambda i: (i, 0))],
        out_shape=[jax.ShapeDtypeStruct((n, d), F32),
                   jax.ShapeDtypeStruct((n, d), F32),
                   jax.ShapeDtypeStruct((n, LANES), jnp.int32),
                   jax.ShapeDtypeStruct((n, LANES), F32)],
        scratch_shapes=[pltpu.VMEM((HALO + tm, dc), F32),
                        pltpu.VMEM((HALO + tm, dd), F32),
                        pltpu.VMEM((tm, dc + dd), BF)],
        compiler_params=_params(("arbitrary",)),
        name="odd_mixer_out_proj_router",
    )(z, z, x, g1, wout, pool_w, pool_scale.reshape(1, dc), sc_w, sc_b.reshape(1, dd),
      ffn_g.reshape(1, d), sc2, sh2, rw)


def _swiglu_up_kernel(h_ref, w1_ref, w3_ref, o_ref, *, rc):
    for r0 in range(0, h_ref.shape[0], rc):
        h = h_ref[r0:r0 + rc, :]
        a = jnp.dot(h, w1_ref[...], preferred_element_type=F32)
        b = jnp.dot(h, w3_ref[...], preferred_element_type=F32)
        o_ref[r0:r0 + rc, :] = (a * jax.nn.sigmoid(a) * b).astype(BF)


def _swiglu_up(h, w1, w3, tm=2048, tn=512):
    n, d = h.shape
    f = w1.shape[1]
    tm = min(tm, n)
    tn = _tile(f, tn)
    return pl.pallas_call(
        functools.partial(_swiglu_up_kernel, rc=min(512, tm)),
        grid=(f // tn, n // tm),
        in_specs=[pl.BlockSpec((tm, d), lambda j, i: (i, 0)),
                  pl.BlockSpec((d, tn), lambda j, i: (0, j)),
                  pl.BlockSpec((d, tn), lambda j, i: (0, j))],
        out_specs=pl.BlockSpec((tm, tn), lambda j, i: (i, j)),
        out_shape=jax.ShapeDtypeStruct((n, f), BF),
        compiler_params=_params(("arbitrary", "arbitrary")),
        name="dense_swiglu_up",
    )(h, w1, w3)


def _down_residual_kernel(a_ref, w_ref, x_ref, g_ref, o_ref):
    y = jnp.dot(a_ref[...], w_ref[...], preferred_element_type=F32)
    o_ref[...] = x_ref[...] + g_ref[0] * y


def _down_residual(a, w, x, gate, seq, tm=1024, tn=512):
    n, k = a.shape
    d = w.shape[1]
    tm = min(tm, seq)
    tn = min(tn, d)
    bpb = seq // tm
    return pl.pallas_call(
        _down_residual_kernel,
        grid=(d // tn, n // tm),
        in_specs=[pl.BlockSpec((tm, k), lambda j, i: (i, 0)),
                  pl.BlockSpec((k, tn), lambda j, i: (0, j)),
                  pl.BlockSpec((tm, tn), lambda j, i: (i, j)),
                  pl.BlockSpec((1, 1, tn), lambda j, i: (i // bpb, 0, j))],
        out_specs=pl.BlockSpec((tm, tn), lambda j, i: (i, j)),
        out_shape=jax.ShapeDtypeStruct((n, d), F32),
        compiler_params=_params(("arbitrary", "arbitrary")),
        name="dense_down_residual",
    )(a, w, x, gate)


def _row_copy(src, dst, sem, src_row, dst_row):
    return pltpu.make_async_copy(src.at[pl.ds(src_row, 1), :], dst.at[pl.ds(dst_row, 1), :], sem)


def _gather_kernel(tok_ref, used_ref, h_hbm, o_ref, buf, sem, *, tm):
    i = pl.program_id(0)

    @pl.when(i < used_ref[0])
    def _():
        def issue(r, carry):
            _row_copy(h_hbm, buf, sem, tok_ref[i * tm + r], r).start()
            return carry

        lax.fori_loop(0, tm, issue, 0)

        def drain(r, carry):
            _row_copy(h_hbm, buf, sem, 0, r).wait()
            return carry

        lax.fori_loop(0, tm, drain, 0)
        o_ref[...] = buf[...].astype(BF)

    @pl.when(i >= used_ref[0])
    def _():
        o_ref[...] = jnp.zeros_like(o_ref)


def _moe_gather(tok_buf, used_rows, h, tm=TM_GATHER):
    rows = tok_buf.shape[0]
    d = h.shape[1]
    used = used_rows // tm
    return pl.pallas_call(
        functools.partial(_gather_kernel, tm=tm),
        grid_spec=pltpu.PrefetchScalarGridSpec(
            num_scalar_prefetch=2,
            grid=(rows // tm,),
            in_specs=[pl.BlockSpec(memory_space=pl.ANY)],
            out_specs=pl.BlockSpec((tm, d), lambda i, tok, used: (i, 0)),
            scratch_shapes=[pltpu.VMEM((tm, d), F32), pltpu.SemaphoreType.DMA(())]),
        out_shape=jax.ShapeDtypeStruct((rows, d), BF),
        compiler_params=_params(("arbitrary",)),
        name="moe_gather",
    )(tok_buf, used, h)


def _moe_up_kernel(be_ref, used_ref, x_ref, w1_ref, w3_ref, o_ref):
    i = pl.program_id(1)

    @pl.when(i < used_ref[0])
    def _():
        x = x_ref[...]
        a = jnp.dot(x, w1_ref[0], preferred_element_type=F32)
        b = jnp.dot(x, w3_ref[0], preferred_element_type=F32)
        o_ref[...] = (a * jax.nn.sigmoid(a) * b).astype(BF)

    @pl.when(i >= used_ref[0])
    def _():
        o_ref[...] = jnp.zeros_like(o_ref)


def _moe_up(blk_expert, used, xs, w1, w3, tm=TM_MOE, tn=1792):
    rows, d = xs.shape
    f = w1.shape[2]
    tn = _tile(f, tn)
    last = lambda i, used: jnp.minimum(i, used[0] - 1)
    return pl.pallas_call(
        _moe_up_kernel,
        grid_spec=pltpu.PrefetchScalarGridSpec(
            num_scalar_prefetch=2,
            grid=(f // tn, rows // tm),
            in_specs=[pl.BlockSpec((tm, d), lambda j, i, be, used: (last(i, used), 0)),
                      pl.BlockSpec((1, d, tn), lambda j, i, be, used: (be[last(i, used)], 0, j)),
                      pl.BlockSpec((1, d, tn), lambda j, i, be, used: (be[last(i, used)], 0, j))],
            out_specs=pl.BlockSpec((tm, tn), lambda j, i, be, used: (i, j))),
        out_shape=jax.ShapeDtypeStruct((rows, f), BF),
        compiler_params=_params(("arbitrary", "arbitrary")),
        name="moe_swiglu_up",
    )(blk_expert, used, xs, w1, w3)


def _moe_down_kernel(be_ref, used_ref, h_ref, w_ref, o_ref):
    i = pl.program_id(1)

    @pl.when(i < used_ref[0])
    def _():
        o_ref[...] = jnp.dot(h_ref[...], w_ref[0], preferred_element_type=F32)

    @pl.when(i >= used_ref[0])
    def _():
        o_ref[...] = jnp.zeros_like(o_ref)


def _moe_down(blk_expert, used, hs, w2, tm=TM_MOE, tn=1024):
    rows, f = hs.shape
    d = w2.shape[2]
    tn = min(tn, d)
    last = lambda i, used: jnp.minimum(i, used[0] - 1)
    return pl.pallas_call(
        _moe_down_kernel,
        grid_spec=pltpu.PrefetchScalarGridSpec(
            num_scalar_prefetch=2,
            grid=(d // tn, rows // tm),
            in_specs=[pl.BlockSpec((tm, f), lambda j, i, be, used: (last(i, used), 0)),
                      pl.BlockSpec((1, f, tn), lambda j, i, be, used: (be[last(i, used)], 0, j))],
            out_specs=pl.BlockSpec((tm, tn), lambda j, i, be, used: (i, j))),
        out_shape=jax.ShapeDtypeStruct((rows, d), F32),
        compiler_params=_params(("arbitrary", "arbitrary")),
        name="moe_down",
    )(blk_expert, used, hs, w2)


def _combine_kernel(pos_ref, ys_hbm, x_ref, tg_ref, g2_ref, o_ref, buf, sem, *, tm):
    i = pl.program_id(0)

    def issue(r, carry):
        for k in range(TOP_K):
            _row_copy(ys_hbm, buf.at[k], sem, pos_ref[(i * tm + r) * TOP_K + k], r).start()
        return carry

    lax.fori_loop(0, tm, issue, 0)

    def drain(r, carry):
        for k in range(TOP_K):
            _row_copy(ys_hbm, buf.at[k], sem, 0, r).wait()
        return carry

    lax.fori_loop(0, tm, drain, 0)
    y = tg_ref[:, 0:1] * buf[0]
    for k in range(1, TOP_K):
        y = y + tg_ref[:, k:k + 1] * buf[k]
    o_ref[...] = x_ref[...] + g2_ref[0] * y


def _moe_combine(pos, ys, x, tg, g2, seq, tm=TM_GATHER):
    n, d = x.shape
    tm = min(tm, seq)
    bpb = seq // tm
    return pl.pallas_call(
        functools.partial(_combine_kernel, tm=tm),
        grid_spec=pltpu.PrefetchScalarGridSpec(
            num_scalar_prefetch=1,
            grid=(n // tm,),
            in_specs=[pl.BlockSpec(memory_space=pl.ANY),
                      pl.BlockSpec((tm, d), lambda i, pos: (i, 0)),
                      pl.BlockSpec((tm, LANES), lambda i, pos: (i, 0)),
                      pl.BlockSpec((1, 1, d), lambda i, pos: (i // bpb, 0, 0))],
            out_specs=pl.BlockSpec((tm, d), lambda i, pos: (i, 0)),
            scratch_shapes=[pltpu.VMEM((TOP_K, tm, d), F32), pltpu.SemaphoreType.DMA(())]),
        out_shape=jax.ShapeDtypeStruct((n, d), F32),
        compiler_params=_params(("arbitrary",)),
        name="moe_combine",
    )(pos, ys, x, tg, g2)


def _dispatch_plan(top_idx, tm):
    n = top_idx.shape[0]
    pairs = n * TOP_K
    e_flat = top_idx.reshape(-1)
    onehot = (e_flat[:, None] == jnp.arange(N_EXPERTS, dtype=jnp.int32)[None, :]).astype(jnp.int32)
    csum = jnp.cumsum(onehot, axis=0)
    counts = csum[-1]
    rank = jnp.take_along_axis(csum, e_flat[:, None], axis=1)[:, 0] - 1
    padded = (counts + tm - 1) // tm * tm
    pend = jnp.cumsum(padded)
    pstart = pend - padded
    start = jnp.cumsum(counts) - counts
    pos = (pstart[e_flat] + rank).astype(jnp.int32)
    order = jnp.argsort(e_flat, stable=True)
    tok_sorted = (order // TOP_K).astype(jnp.int32)
    nb = pairs // tm + N_EXPERTS
    rows = nb * tm
    used_rows = pend[-1].astype(jnp.int32)
    used = used_rows // tm
    blk_start = jnp.arange(nb, dtype=jnp.int32) * tm
    blk_expert = jnp.minimum(jnp.sum(blk_start[:, None] >= pend[None, :], axis=1), N_EXPERTS - 1)
    blk_expert = jnp.where(jnp.arange(nb) < used, blk_expert, blk_expert[used - 1]).astype(jnp.int32)
    r = jnp.arange(rows, dtype=jnp.int32)
    e_r = blk_expert[r // tm]
    idx_in = r - pstart[e_r]
    valid = (idx_in < counts[e_r]) & (r < used_rows)
    src = jnp.clip(start[e_r] + idx_in, 0, pairs - 1)
    tok_buf = jnp.where(valid, tok_sorted[src], 0).astype(jnp.int32)
    return pos, tok_buf, blk_expert, used.reshape(1).astype(jnp.int32), used_rows.reshape(1)


def _moe(h2, ti, tg, x, g2, w1, w3, w2, seq):
    pos, tok_buf, blk_expert, used, used_rows = _dispatch_plan(ti[:, :TOP_K], TM_MOE)
    xs = _moe_gather(tok_buf, used_rows, h2)
    hs = _moe_up(blk_expert, used, xs, w1, w3)
    ys = _moe_down(blk_expert, used, hs, w2)
    return _moe_combine(pos, ys, x, tg, g2, seq)


def _final_norm_kernel(x_ref, g_ref, o_ref):
    o_ref[...] = _rms(x_ref[...], g_ref[...])


def _final_norm(x, g, tm=512):
    n, d = x.shape
    tm = min(tm, n)
    return pl.pallas_call(
        _final_norm_kernel,
        grid=(n // tm,),
        in_specs=[pl.BlockSpec((tm, d), lambda i: (i, 0)), _resident((1, d), lambda i: (0, 0))],
        out_specs=pl.BlockSpec((tm, d), lambda i: (i, 0)),
        out_shape=jax.ShapeDtypeStruct((n, d), F32),
        compiler_params=_params(("arbitrary",)),
        name="final_norm",
    )(x, g.reshape(1, d))


def kernel(x, c, ada_w, ada_b, norm_mix_g, norm_ffn_g, final_g, ev_w_in, ev_w_out, gm_ws, gm_b, gm_norm_g, cv_w, cv_b, cv_ln_g, cv_ln_b, ffn_w1, ffn_w3, ffn_w2, od_w_in, od_w_out, pool_w, pool_scale, sc_w, sc_b, router_w, moe_w1, moe_w3, moe_w2):
    bsz, seq, d = x.shape
    depth = ada_b.shape[0]
    xf = x.reshape(bsz * seq, d)
    mod = _ada(c, ada_w, ada_b)
    for l in range(depth):
        m = mod[l].reshape(bsz, 6, 1, d)
        sh1, sc1, g1, sh2, sc2, g2 = [m[:, k] for k in range(6)]
        i = l // 2
        if l % 2 == 0:
            z = _norm_mm(xf, norm_mix_g[l], sc1, sh1, ev_w_in[i].astype(BF), seq)
            xf, h2 = _even_mix(z, xf, g1, ev_w_out[i].astype(BF), gm_ws[i], gm_b[i], gm_norm_g[i],
                               cv_w[i], cv_b[i], cv_ln_g[i], cv_ln_b[i], norm_ffn_g[l], sc2, sh2, seq)
            hs = _swiglu_up(h2, ffn_w1[i].astype(BF), ffn_w3[i].astype(BF))
            xf = _down_residual(hs, ffn_w2[i].astype(BF), xf, g2, seq)
        else:
            z = _norm_mm(xf, norm_mix_g[l], sc1, sh1, od_w_in[i].astype(BF), seq)
            xf, h2, ti, tg = _odd_mix(z, xf, g1, od_w_out[i].astype(BF), pool_w[i].astype(BF), pool_scale[i],
                                      sc_w[i], sc_b[i], norm_ffn_g[l], sc2, sh2, router_w[i], seq)
            xf = _moe(h2, ti, tg, xf, g2, moe_w1[i].astype(BF), moe_w3[i].astype(BF), moe_w2[i].astype(BF), seq)
    return _final_norm(xf, final_g).reshape(bsz, seq, d)
```

```python
import functools

import jax
import jax.numpy as jnp
from jax import lax
from jax.experimental import pallas as pl
from jax.experimental.pallas import tpu as pltpu

BF = jnp.bfloat16
F32 = jnp.float32
EPS = 1e-6

CHUNK = 64
GMLP_BLOCK = 128
N_GROUPS_A = 4
CONV_B = 31
CONV_D = 3
POOL_WINDOWS = (2, 4, 8, 16)
N_EXPERTS = 8
TOP_K = 2

LANES = 128
HALO = 32
VMEM_LIMIT = 56 * 1024 * 1024

TM_MIX = 256
TM_IN = 512
TM_MOE = 512
TM_COMBINE = 256
DMA_UNROLL = 8


def _params(sem):
    return pltpu.CompilerParams(dimension_semantics=sem, vmem_limit_bytes=VMEM_LIMIT)


def _resident(shape, index_map):
    return pl.BlockSpec(shape, index_map, pipeline_mode=pl.Buffered(1))


def _tile(n, want):
    t = min(want, n)
    while n % t:
        t -= LANES
    return t


def _rms(x, g):
    return x * lax.rsqrt(jnp.mean(x * x, axis=-1, keepdims=True) + EPS) * g


def _ada_kernel(c_ref, w_ref, b_ref, o_ref):
    c = c_ref[...]
    s = c * jax.nn.sigmoid(c)
    base = jnp.dot(s, w_ref[...], preferred_element_type=F32, precision=lax.Precision.HIGHEST)
    o_ref[...] = base[None, :, :] + b_ref[...][:, None, :]


def _ada(c, ada_w, ada_b, tn=1024):
    bsz, d = c.shape
    depth, n6 = ada_b.shape
    tn = _tile(n6, tn)
    return pl.pallas_call(
        _ada_kernel,
        grid=(n6 // tn,),
        in_specs=[pl.BlockSpec((bsz, d), lambda j: (0, 0)),
                  pl.BlockSpec((d, tn), lambda j: (0, j)),
                  pl.BlockSpec((depth, tn), lambda j: (0, j))],
        out_specs=pl.BlockSpec((depth, bsz, tn), lambda j: (0, 0, j)),
        out_shape=jax.ShapeDtypeStruct((depth, bsz, n6), F32),
        compiler_params=_params(("arbitrary",)),
        name="ada_mod",
    )(c, ada_w, ada_b)


def _norm_mm_kernel(x_ref, g_ref, sc_ref, sh_ref, w_ref, z_ref, *, tn):
    h = _rms(x_ref[...], g_ref[...]) * (1.0 + sc_ref[0]) + sh_ref[0]
    hb = h.astype(BF)
    nout = z_ref.shape[1]
    for n0 in range(0, nout, tn):
        z_ref[:, n0:n0 + tn] = jnp.dot(hb, w_ref[:, n0:n0 + tn], preferred_element_type=F32)


def _norm_mm(x, g, sc, sh, w, layer, seq, tm=TM_IN, tn=512):
    n, d = x.shape
    nout = w.shape[2]
    tm = min(tm, seq)
    bpb = seq // tm
    return pl.pallas_call(
        functools.partial(_norm_mm_kernel, tn=min(tn, nout)),
        grid=(n // tm,),
        in_specs=[pl.BlockSpec((tm, d), lambda i: (i, 0)),
                  _resident((1, d), lambda i: (0, 0)),
                  pl.BlockSpec((1, 1, d), lambda i: (i // bpb, 0, 0)),
                  pl.BlockSpec((1, 1, d), lambda i: (i // bpb, 0, 0)),
                  _resident((None, d, nout), lambda i: (layer, 0, 0))],
        out_specs=pl.BlockSpec((tm, nout), lambda i: (i, 0)),
        out_shape=jax.ShapeDtypeStruct((n, nout), F32),
        compiler_params=_params(("arbitrary",)),
        name="norm_in_proj",
    )(x, g.reshape(1, d), sc, sh, w)


def _out_proj_residual(ycat, x_ref, g1_ref, wout_ref, xo_ref, tn=512):
    d = xo_ref.shape[1]
    yc = ycat[...]
    for n0 in range(0, d, tn):
        y = jnp.dot(yc, wout_ref[:, n0:n0 + tn], preferred_element_type=F32)
        xo_ref[:, n0:n0 + tn] = x_ref[:, n0:n0 + tn] + g1_ref[0][:, n0:n0 + tn] * y


def _ffn_input(xo_ref, fg_ref, sc2_ref, sh2_ref):
    return _rms(xo_ref[...], fg_ref[...]) * (1.0 + sc2_ref[0]) + sh2_ref[0]


def _even_mix_kernel(z_ref, halo_ref, x_ref, g1_ref, wout_ref, ws_ref, gmb_ref, gng_ref,
                     cvw_ref, cvb_ref, lng_ref, lnb_ref, fg_ref, sc2_ref, sh2_ref,
                     xo_ref, h2_ref, gbuf, conv, ycat, *, tm, da, db, bpb):
    i = pl.program_id(0)
    dg = da // N_GROUPS_A

    r = lax.broadcasted_iota(jnp.int32, (GMLP_BLOCK, GMLP_BLOCK), 0) // CHUNK
    c = lax.broadcasted_iota(jnp.int32, (GMLP_BLOCK, GMLP_BLOCK), 1) // CHUNK
    causal = c <= r
    ws = [jnp.where(causal, ws_ref[g], 0.0).astype(BF) for g in range(N_GROUPS_A)]
    for n in range(tm // GMLP_BLOCK):
        rows = slice(n * GMLP_BLOCK, (n + 1) * GMLP_BLOCK)
        v = jax.nn.gelu(z_ref[rows, da:2 * da], approximate=True)
        vb = _rms(v, gng_ref[...]).astype(BF)
        for g in range(N_GROUPS_A):
            cols = slice(g * dg, (g + 1) * dg)
            sp = jnp.dot(ws[g], vb[:, cols], preferred_element_type=F32) + gmb_ref[:, g:g + 1]
            u = jax.nn.gelu(z_ref[rows, cols], approximate=True)
            ycat[rows, cols] = (u * sp).astype(BF)

    a = z_ref[:, 2 * da:2 * da + db]
    gate = z_ref[:, 2 * da + db:]
    gbuf[HALO:HALO + tm, :] = a * jax.nn.sigmoid(gate)
    hglu = halo_ref[:, 0:db] * jax.nn.sigmoid(halo_ref[:, db:2 * db])
    gbuf[0:HALO, :] = jnp.where(i % bpb == 0, 0.0, hglu)

    rc = min(64, tm)
    off = HALO - (CONV_B - 1)

    def conv_lanes(cb, carry):
        lanes = pl.ds(pl.multiple_of(cb * LANES, LANES), LANES)
        wk = [cvw_ref[k:k + 1, lanes] for k in range(CONV_B)]
        bias = cvb_ref[:, lanes]
        for r0 in range(0, tm, rc):
            acc = wk[0] * gbuf[pl.ds(off + r0, rc), lanes]
            for k in range(1, CONV_B):
                acc = acc + wk[k] * gbuf[pl.ds(off + k + r0, rc), lanes]
            conv[pl.ds(r0, rc), lanes] = acc + bias
        return carry

    lax.fori_loop(0, db // LANES, conv_lanes, 0)

    cv = conv[...]
    mu = jnp.mean(cv, axis=-1, keepdims=True)
    xc = cv - mu
    ln = xc * lax.rsqrt(jnp.mean(xc * xc, axis=-1, keepdims=True) + EPS) * lng_ref[...] + lnb_ref[...]
    ycat[:, da:da + db] = (ln * jax.nn.sigmoid(ln)).astype(BF)

    _out_proj_residual(ycat, x_ref, g1_ref, wout_ref, xo_ref)
    h2_ref[...] = _ffn_input(xo_ref, fg_ref, sc2_ref, sh2_ref).astype(h2_ref.dtype)


def _even_mix(z, x, g1, wout, layer, gm_ws, gm_b, gm_norm_g, cv_w, cv_b, cv_ln_g, cv_ln_b,
              ffn_g, sc2, sh2, seq, tm=TM_MIX):
    n, d = x.shape
    da = gm_norm_g.shape[0]
    db = cv_b.shape[0]
    tm = min(tm, seq)
    bpb = seq // tm
    hb = tm // HALO
    vec = lambda k: _resident((1, k), lambda i: (0, 0))
    per_batch = pl.BlockSpec((1, 1, d), lambda i: (i // bpb, 0, 0))
    return pl.pallas_call(
        functools.partial(_even_mix_kernel, tm=tm, da=da, db=db, bpb=bpb),
        grid=(n // tm,),
        in_specs=[pl.BlockSpec((tm, 2 * da + 2 * db), lambda i: (i, 0)),
                  pl.BlockSpec((HALO, 2 * db), lambda i: (jnp.maximum(i * hb - 1, 0), da // db)),
                  pl.BlockSpec((tm, d), lambda i: (i, 0)),
                  per_batch,
                  _resident((None, da + db, d), lambda i: (layer, 0, 0)),
                  _resident((N_GROUPS_A, GMLP_BLOCK, GMLP_BLOCK), lambda i: (0, 0, 0)),
                  _resident((GMLP_BLOCK, N_GROUPS_A), lambda i: (0, 0)),
                  vec(da),
                  _resident((CONV_B, db), lambda i: (0, 0)),
                  vec(db), vec(db), vec(db), vec(d),
                  per_batch, per_batch],
        out_specs=[pl.BlockSpec((tm, d), lambda i: (i, 0)),
                   pl.BlockSpec((tm, d), lambda i: (i, 0))],
        out_shape=[jax.ShapeDtypeStruct((n, d), F32),
                   jax.ShapeDtypeStruct((n, d), BF)],
        scratch_shapes=[pltpu.VMEM((HALO + tm, db), F32),
                        pltpu.VMEM((tm, db), F32),
                        pltpu.VMEM((tm, da + db), BF)],
        compiler_params=_params(("arbitrary",)),
        name="even_mixer_out_proj",
    )(z, z, x, g1, wout, gm_ws, gm_b.T, gm_norm_g.reshape(1, da), cv_w, cv_b.reshape(1, db),
      cv_ln_g.reshape(1, db), cv_ln_b.reshape(1, db), ffn_g.reshape(1, d), sc2, sh2)


def _odd_mix_kernel(z_ref, halo_ref, x_ref, g1_ref, wout_ref, pw_ref, ps_ref, scw_ref, scb_ref,
                    fg_ref, sc2_ref, sh2_ref, rw_ref,
                    xo_ref, h2_ref, ti_ref, tg_ref, pbuf, qbuf, ycat, *, tm, dc, dd, bpb):
    i = pl.program_id(0)
    first = i % bpb == 0
    dgc = dc // len(POOL_WINDOWS)

    pbuf[HALO:HALO + tm, :] = z_ref[:, 0:dc]
    pbuf[0:HALO, :] = jnp.where(first, 0.0, halo_ref[:, 0:dc])
    pos = (i % bpb) * tm + lax.broadcasted_iota(jnp.int32, (tm, 1), 0)
    for g, w in enumerate(POOL_WINDOWS):
        cols = slice(g * dgc, (g + 1) * dgc)
        tok = pbuf[HALO:HALO + tm, cols]
        acc = tok
        for dlt in range(1, w):
            acc = acc + pbuf[HALO - dlt:HALO - dlt + tm, cols]
        cnt = jnp.minimum(pos + 1, w).astype(F32)
        diff = (acc / cnt - tok).astype(BF)
        mixed = jnp.dot(diff, pw_ref[g], preferred_element_type=F32) * ps_ref[:, cols]
        ycat[:, cols] = mixed.astype(BF)

    qbuf[HALO:HALO + tm, :] = z_ref[:, dc + dd:dc + 2 * dd] * z_ref[:, dc + 2 * dd:]
    hp = halo_ref[:, dc + dd:dc + 2 * dd] * halo_ref[:, dc + 2 * dd:]
    qbuf[0:HALO, :] = jnp.where(first, 0.0, hp)
    off = HALO - (CONV_D - 1)
    y = scw_ref[0:1, :] * qbuf[off:off + tm, :]
    for k in range(1, CONV_D):
        y = y + scw_ref[k:k + 1, :] * qbuf[off + k:off + k + tm, :]
    y = y + scb_ref[...]
    ycat[:, dc:dc + dd] = (z_ref[:, dc:dc + dd] * y).astype(BF)

    _out_proj_residual(ycat, x_ref, g1_ref, wout_ref, xo_ref)
    h2 = _ffn_input(xo_ref, fg_ref, sc2_ref, sh2_ref)
    h2_ref[...] = h2

    hh = h2.astype(BF)
    hl = (h2 - hh.astype(F32)).astype(BF)
    p = jnp.dot(hh, rw_ref[...], preferred_element_type=F32)
    q = jnp.dot(hl, rw_ref[:, 0:LANES], preferred_element_type=F32)
    logits = p[:, 0:LANES] + p[:, LANES:] + q

    lane = lax.broadcasted_iota(jnp.int32, logits.shape, 1)
    lane_f = lane.astype(F32)
    neg = jnp.float32(-jnp.inf)
    l1 = jnp.where(lane < N_EXPERTS, logits, neg)
    m1 = jnp.max(l1, axis=-1, keepdims=True)
    i1 = jnp.min(jnp.where(l1 == m1, lane_f, float(LANES)), axis=-1, keepdims=True)
    l2 = jnp.where(lane_f == i1, neg, l1)
    m2 = jnp.max(l2, axis=-1, keepdims=True)
    i2 = jnp.min(jnp.where(l2 == m2, lane_f, float(LANES)), axis=-1, keepdims=True)
    e = jnp.exp(m2 - m1)
    ga = 1.0 / (1.0 + e)
    gb = e / (1.0 + e)
    ti_ref[...] = jnp.where(lane == 0, i1, jnp.where(lane == 1, i2, 0.0)).astype(jnp.int32)
    tg_ref[...] = jnp.where(lane == 0, ga, jnp.where(lane == 1, gb, 0.0))


def _odd_mix(z, x, g1, wout, layer, pool_w, pool_scale, sc_w, sc_b, ffn_g, sc2, sh2, router_w, seq, tm=TM_MIX):
    n, d = x.shape
    dc = pool_scale.shape[0]
    dd = sc_b.shape[0]
    ng = len(POOL_WINDOWS)
    dgc = dc // ng
    tm = min(tm, seq)
    bpb = seq // tm
    hb = tm // HALO
    rw_hi = router_w.astype(BF)
    rw_lo = (router_w - rw_hi.astype(F32)).astype(BF)
    rw = jnp.zeros((d, 2 * LANES), BF).at[:, :N_EXPERTS].set(rw_hi).at[:, LANES:LANES + N_EXPERTS].set(rw_lo)
    vec = lambda k: _resident((1, k), lambda i: (0, 0))
    per_batch = pl.BlockSpec((1, 1, d), lambda i: (i // bpb, 0, 0))
    return pl.pallas_call(
        functools.partial(_odd_mix_kernel, tm=tm, dc=dc, dd=dd, bpb=bpb),
        grid=(n // tm,),
        in_specs=[pl.BlockSpec((tm, dc + 3 * dd), lambda i: (i, 0)),
                  pl.BlockSpec((HALO, dc + 3 * dd), lambda i: (jnp.maximum(i * hb - 1, 0), 0)),
                  pl.BlockSpec((tm, d), lambda i: (i, 0)),
                  per_batch,
                  _resident((None, dc + dd, d), lambda i: (layer, 0, 0)),
                  _resident((ng, dgc, dgc), lambda i: (0, 0, 0)),
                  vec(dc),
                  _resident((CONV_D, dd), lambda i: (0, 0)),
                  vec(dd), vec(d),
                  per_batch, per_batch,
                  _resident((d, 2 * LANES), lambda i: (0, 0))],
        out_specs=[pl.BlockSpec((tm, d), lambda i: (i, 0)),
                   pl.BlockSpec((tm, d), lambda i: (i, 0)),
                   pl.BlockSpec((tm, LANES), lambda i: (i, 0)),
                   pl.BlockSpec((tm, LANES), lambda i: (i, 0))],
        out_shape=[jax.ShapeDtypeStruct((n, d), F32),
                   jax.ShapeDtypeStruct((n, d), F32),
                   jax.ShapeDtypeStruct((n, LANES), jnp.int32),
                   jax.ShapeDtypeStruct((n, LANES), F32)],
        scratch_shapes=[pltpu.VMEM((HALO + tm, dc), F32),
                        pltpu.VMEM((HALO + tm, dd), F32),
                        pltpu.VMEM((tm, dc + dd), BF)],
        compiler_params=_params(("arbitrary",)),
        name="odd_mixer_out_proj_router",
    )(z, z, x, g1, wout, pool_w, pool_scale.reshape(1, dc), sc_w, sc_b.reshape(1, dd),
      ffn_g.reshape(1, d), sc2, sh2, rw)


def _swiglu_up_kernel(h_ref, w1_ref, w3_ref, o_ref, w1b, w3b, *, rc):
    @pl.when(pl.program_id(1) == 0)
    def _():
        w1b[...] = w1_ref[...].astype(BF)
        w3b[...] = w3_ref[...].astype(BF)

    for r0 in range(0, h_ref.shape[0], rc):
        h = h_ref[r0:r0 + rc, :]
        a = jnp.dot(h, w1b[...], preferred_element_type=F32)
        b = jnp.dot(h, w3b[...], preferred_element_type=F32)
        o_ref[r0:r0 + rc, :] = (a * jax.nn.sigmoid(a) * b).astype(BF)


def _swiglu_up(h, w1, w3, layer, tm=2048, tn=512):
    n, d = h.shape
    f = w1.shape[2]
    tm = min(tm, n)
    tn = _tile(f, tn)
    wspec = pl.BlockSpec((None, d, tn), lambda j, i: (layer, 0, j))
    return pl.pallas_call(
        functools.partial(_swiglu_up_kernel, rc=min(512, tm)),
        grid=(f // tn, n // tm),
        in_specs=[pl.BlockSpec((tm, d), lambda j, i: (i, 0)), wspec, wspec],
        out_specs=pl.BlockSpec((tm, tn), lambda j, i: (i, j)),
        out_shape=jax.ShapeDtypeStruct((n, f), BF),
        scratch_shapes=[pltpu.VMEM((d, tn), BF), pltpu.VMEM((d, tn), BF)],
        compiler_params=_params(("arbitrary", "arbitrary")),
        name="dense_swiglu_up",
    )(h, w1, w3)


def _down_residual_kernel(a_ref, w_ref, x_ref, g_ref, o_ref):
    y = jnp.dot(a_ref[...], w_ref[...], preferred_element_type=F32)
    o_ref[...] = x_ref[...] + g_ref[0] * y


def _down_residual(a, w, layer, x, gate, seq, tm=1024, tn=512):
    n, k = a.shape
    d = w.shape[2]
    tm = min(tm, seq)
    tn = min(tn, d)
    bpb = seq // tm
    return pl.pallas_call(
        _down_residual_kernel,
        grid=(d // tn, n // tm),
        in_specs=[pl.BlockSpec((tm, k), lambda j, i: (i, 0)),
                  pl.BlockSpec((None, k, tn), lambda j, i: (layer, 0, j)),
                  pl.BlockSpec((tm, tn), lambda j, i: (i, j)),
                  pl.BlockSpec((1, 1, tn), lambda j, i: (i // bpb, 0, j))],
        out_specs=pl.BlockSpec((tm, tn), lambda j, i: (i, j)),
        out_shape=jax.ShapeDtypeStruct((n, d), F32),
        compiler_params=_params(("arbitrary", "arbitrary")),
        name="dense_down_residual",
    )(a, w, x, gate)


def _row_copy(src, dst, sem, src_row, dst_row):
    return pltpu.make_async_copy(src.at[pl.ds(src_row, 1), :], dst.at[pl.ds(dst_row, 1), :], sem)


def _gather_kernel(tok_ref, base_ref, lim_ref, used_ref, h_hbm, o_ref, buf, sem, *, tm):
    i = pl.program_id(0)
    used = used_ref[0]

    def issue(blk, slot):
        base = base_ref[blk]
        lim = lim_ref[blk]

        def body(r, carry):
            tok = tok_ref[jnp.minimum(base + r, lim)]
            _row_copy(h_hbm, buf.at[slot], sem.at[slot], tok, r).start()
            return carry

        lax.fori_loop(0, tm, body, 0, unroll=DMA_UNROLL)

    @pl.when(i == 0)
    def _():
        issue(0, 0)

    @pl.when(i + 1 < used)
    def _():
        issue(i + 1, (i + 1) % 2)

    @pl.when(i < used)
    def _():
        slot = i % 2
        pltpu.make_async_copy(h_hbm.at[pl.ds(0, tm), :], buf.at[slot], sem.at[slot]).wait()
        o_ref[...] = buf[slot].astype(BF)

    @pl.when(i >= used)
    def _():
        o_ref[...] = jnp.zeros_like(o_ref)


def _moe_gather(plan, h, tm):
    rows = plan["nb"] * tm
    d = h.shape[1]
    return pl.pallas_call(
        functools.partial(_gather_kernel, tm=tm),
        grid_spec=pltpu.PrefetchScalarGridSpec(
            num_scalar_prefetch=4,
            grid=(plan["nb"],),
            in_specs=[pl.BlockSpec(memory_space=pl.ANY)],
            out_specs=pl.BlockSpec((tm, d), lambda i, *_: (i, 0)),
            scratch_shapes=[pltpu.VMEM((2, tm, d), F32), pltpu.SemaphoreType.DMA((2,))]),
        out_shape=jax.ShapeDtypeStruct((rows, d), BF),
        compiler_params=_params(("arbitrary",)),
        name="moe_gather",
    )(plan["tok_sorted"], plan["base"], plan["lim"], plan["used"], h)


def _moe_up_kernel(be_ref, first_ref, used_ref, x_ref, w1_ref, w3_ref, o_ref, w1b, w3b):
    i = pl.program_id(1)

    @pl.when(first_ref[i] == 1)
    def _():
        w1b[...] = w1_ref[...].astype(BF)
        w3b[...] = w3_ref[...].astype(BF)

    @pl.when(i < used_ref[0])
    def _():
        x = x_ref[...]
        a = jnp.dot(x, w1b[...], preferred_element_type=F32)
        b = jnp.dot(x, w3b[...], preferred_element_type=F32)
        o_ref[...] = (a * jax.nn.sigmoid(a) * b).astype(BF)

    @pl.when(i >= used_ref[0])
    def _():
        o_ref[...] = jnp.zeros_like(o_ref)


def _moe_up(plan, xs, w1, w3, layer, tm, tn=1024):
    rows, d = xs.shape
    f = w1.shape[3]
    tn = _tile(f, tn)
    last = lambda i, used: jnp.minimum(i, used[0] - 1)
    wspec = pl.BlockSpec((None, None, d, tn), lambda j, i, be, first, used: (layer, be[last(i, used)], 0, j))
    return pl.pallas_call(
        _moe_up_kernel,
        grid_spec=pltpu.PrefetchScalarGridSpec(
            num_scalar_prefetch=3,
            grid=(f // tn, rows // tm),
            in_specs=[pl.BlockSpec((tm, d), lambda j, i, be, first, used: (last(i, used), 0)), wspec, wspec],
            out_specs=pl.BlockSpec((tm, tn), lambda j, i, be, first, used: (i, j)),
            scratch_shapes=[pltpu.VMEM((d, tn), BF), pltpu.VMEM((d, tn), BF)]),
        out_shape=jax.ShapeDtypeStruct((rows, f), BF),
        compiler_params=_params(("arbitrary", "arbitrary")),
        name="moe_swiglu_up",
    )(plan["blk_expert"], plan["first"], plan["used"], xs, w1, w3)


def _moe_down_kernel(be_ref, used_ref, h_ref, w_ref, o_ref):
    i = pl.program_id(1)

    @pl.when(i < used_ref[0])
    def _():
        o_ref[...] = jnp.dot(h_ref[...], w_ref[...], preferred_element_type=F32)

    @pl.when(i >= used_ref[0])
    def _():
        o_ref[...] = jnp.zeros_like(o_ref)


def _moe_down(plan, hs, w2, layer, tm, tn=1024):
    rows, f = hs.shape
    d = w2.shape[3]
    tn = min(tn, d)
    last = lambda i, used: jnp.minimum(i, used[0] - 1)
    return pl.pallas_call(
        _moe_down_kernel,
        grid_spec=pltpu.PrefetchScalarGridSpec(
            num_scalar_prefetch=2,
            grid=(d // tn, rows // tm),
            in_specs=[pl.BlockSpec((tm, f), lambda j, i, be, used: (last(i, used), 0)),
                      pl.BlockSpec((None, None, f, tn), lambda j, i, be, used: (layer, be[last(i, used)], 0, j))],
            out_specs=pl.BlockSpec((tm, tn), lambda j, i, be, used: (i, j))),
        out_shape=jax.ShapeDtypeStruct((rows, d), F32),
        compiler_params=_params(("arbitrary", "arbitrary")),
        name="moe_down",
    )(plan["blk_expert"], plan["used"], hs, w2)


def _combine_kernel(pos_ref, ys_hbm, x_ref, tg_ref, g2_ref, fin_ref, o_ref, buf, sem, *, tm, final):
    i = pl.program_id(0)
    steps = pl.num_programs(0)

    def issue(tile, slot):
        def body(r, carry):
            for k in range(TOP_K):
                src = pos_ref[(tile * tm + r) * TOP_K + k]
                _row_copy(ys_hbm, buf.at[slot], sem.at[slot], src, k * tm + r).start()
            return carry

        lax.fori_loop(0, tm, body, 0, unroll=DMA_UNROLL)

    @pl.when(i == 0)
    def _():
        issue(0, 0)

    @pl.when(i + 1 < steps)
    def _():
        issue(i + 1, (i + 1) % 2)

    slot = i % 2
    pltpu.make_async_copy(ys_hbm.at[pl.ds(0, TOP_K * tm), :], buf.at[slot], sem.at[slot]).wait()
    y = tg_ref[:, 0:1] * buf[slot, 0:tm, :]
    for k in range(1, TOP_K):
        y = y + tg_ref[:, k:k + 1] * buf[slot, k * tm:(k + 1) * tm, :]
    res = x_ref[...] + g2_ref[0] * y
    o_ref[...] = _rms(res, fin_ref[...]) if final else res


def _moe_combine(pos, ys, x, tg, g2, final_g, final, seq, tm=TM_COMBINE):
    n, d = x.shape
    tm = min(tm, seq)
    bpb = seq // tm
    return pl.pallas_call(
        functools.partial(_combine_kernel, tm=tm, final=final),
        grid_spec=pltpu.PrefetchScalarGridSpec(
            num_scalar_prefetch=1,
            grid=(n // tm,),
            in_specs=[pl.BlockSpec(memory_space=pl.ANY),
                      pl.BlockSpec((tm, d), lambda i, pos: (i, 0)),
                      pl.BlockSpec((tm, LANES), lambda i, pos: (i, 0)),
                      pl.BlockSpec((1, 1, d), lambda i, pos: (i // bpb, 0, 0)),
                      _resident((1, d), lambda i, pos: (0, 0))],
            out_specs=pl.BlockSpec((tm, d), lambda i, pos: (i, 0)),
            scratch_shapes=[pltpu.VMEM((2, TOP_K * tm, d), F32), pltpu.SemaphoreType.DMA((2,))]),
        out_shape=jax.ShapeDtypeStruct((n, d), F32),
        compiler_params=_params(("arbitrary",)),
        name="moe_combine",
    )(pos, ys, x, tg, g2, final_g.reshape(1, d))


def _dispatch_plan(top_idx, tm):
    n = top_idx.shape[0]
    pairs = n * TOP_K
    e_flat = top_idx.reshape(-1)
    eids = jnp.arange(N_EXPERTS, dtype=jnp.int32)
    onehot = (e_flat[:, None] == eids[None, :]).astype(jnp.int32)
    csum = jnp.cumsum(onehot, axis=0)
    counts = csum[-1]
    padded = (counts + tm - 1) // tm * tm
    pend = jnp.cumsum(padded)
    pstart = pend - padded
    start = jnp.cumsum(counts) - counts
    pos = jnp.sum(onehot * (csum - 1 + pstart[None, :]), axis=1).astype(jnp.int32)
    order = jnp.argsort(e_flat, stable=True)
    tok_sorted = (order // TOP_K).astype(jnp.int32)
    nb = pairs // tm + N_EXPERTS
    used = (pend[-1] // tm).astype(jnp.int32)
    blk_start = jnp.arange(nb, dtype=jnp.int32) * tm
    be = jnp.minimum(jnp.sum(blk_start[:, None] >= pend[None, :], axis=1), N_EXPERTS - 1)
    be_last = jnp.minimum(jnp.sum((used - 1) * tm >= pend), N_EXPERTS - 1)
    be = jnp.where(jnp.arange(nb) < used, be, be_last).astype(jnp.int32)
    sel = (be[:, None] == eids[None, :]).astype(jnp.int32)
    start_b = jnp.sum(sel * start[None, :], axis=1)
    base = (start_b + blk_start - jnp.sum(sel * pstart[None, :], axis=1)).astype(jnp.int32)
    lim = (start_b + jnp.sum(sel * counts[None, :], axis=1) - 1).astype(jnp.int32)
    first = jnp.concatenate([jnp.ones((1,), jnp.int32), (be[1:] != be[:-1]).astype(jnp.int32)])
    return dict(pos=pos, tok_sorted=tok_sorted, blk_expert=be, first=first, base=base, lim=lim,
                used=used.reshape(1), nb=nb)


def _moe(h2, ti, tg, x, g2, w1, w3, w2, layer, final_g, final, seq):
    plan = _dispatch_plan(ti[:, :TOP_K], TM_MOE)
    xs = _moe_gather(plan, h2, TM_MOE)
    hs = _moe_up(plan, xs, w1, w3, layer, TM_MOE)
    ys = _moe_down(plan, hs, w2, layer, TM_MOE)
    return _moe_combine(plan["pos"], ys, x, tg, g2, final_g, final, seq)


def _final_norm_kernel(x_ref, g_ref, o_ref):
    o_ref[...] = _rms(x_ref[...], g_ref[...])


def _final_norm(x, g, tm=512):
    n, d = x.shape
    tm = min(tm, n)
    return pl.pallas_call(
        _final_norm_kernel,
        grid=(n // tm,),
        in_specs=[pl.BlockSpec((tm, d), lambda i: (i, 0)), _resident((1, d), lambda i: (0, 0))],
        out_specs=pl.BlockSpec((tm, d), lambda i: (i, 0)),
        out_shape=jax.ShapeDtypeStruct((n, d), F32),
        compiler_params=_params(("arbitrary",)),
        name="final_norm",
    )(x, g.reshape(1, d))


def kernel(x, c, ada_w, ada_b, norm_mix_g, norm_ffn_g, final_g, ev_w_in, ev_w_out, gm_ws, gm_b, gm_norm_g, cv_w, cv_b, cv_ln_g, cv_ln_b, ffn_w1, ffn_w3, ffn_w2, od_w_in, od_w_out, pool_w, pool_scale, sc_w, sc_b, router_w, moe_w1, moe_w3, moe_w2):
    bsz, seq, d = x.shape
    depth = ada_b.shape[0]
    xf = x.reshape(bsz * seq, d)
    mod = _ada(c, ada_w, ada_b)
    ev_w_in_b, ev_w_out_b = ev_w_in.astype(BF), ev_w_out.astype(BF)
    od_w_in_b, od_w_out_b = od_w_in.astype(BF), od_w_out.astype(BF)
    ffn_w2_b, moe_w2_b, pool_w_b = ffn_w2.astype(BF), moe_w2.astype(BF), pool_w.astype(BF)
    for l in range(depth):
        m = mod[l].reshape(bsz, 6, 1, d)
        sh1, sc1, g1, sh2, sc2, g2 = [m[:, k] for k in range(6)]
        i = l // 2
        last = l == depth - 1
        if l % 2 == 0:
            z = _norm_mm(xf, norm_mix_g[l], sc1, sh1, ev_w_in_b, i, seq)
            xf, h2 = _even_mix(z, xf, g1, ev_w_out_b, i, gm_ws[i], gm_b[i], gm_norm_g[i],
                               cv_w[i], cv_b[i], cv_ln_g[i], cv_ln_b[i], norm_ffn_g[l], sc2, sh2, seq)
            hs = _swiglu_up(h2, ffn_w1, ffn_w3, i)
            xf = _down_residual(hs, ffn_w2_b, i, xf, g2, seq)
            if last:
                xf = _final_norm(xf, final_g)
        else:
            z = _norm_mm(xf, norm_mix_g[l], sc1, sh1, od_w_in_b, i, seq)
            xf, h2, ti, tg = _odd_mix(z, xf, g1, od_w_out_b, i, pool_w_b[i], pool_scale[i],
                                      sc_w[i], sc_b[i], norm_ffn_g[l], sc2, sh2, router_w[i], seq)
            xf = _moe(h2, ti, tg, xf, g2, moe_w1, moe_w3, moe_w2_b, i, final_g, last, seq)
    return xf.reshape(bsz, seq, d)
```

```python
import functools

import jax
import jax.numpy as jnp
from jax import lax
from jax.experimental import pallas as pl
from jax.experimental.pallas import tpu as pltpu

BF = jnp.bfloat16
F32 = jnp.float32
EPS = 1e-6

CHUNK = 64
GMLP_BLOCK = 128
N_GROUPS_A = 4
CONV_B = 31
CONV_D = 3
POOL_WINDOWS = (2, 4, 8, 16)
N_EXPERTS = 8
TOP_K = 2

LANES = 128
SUBLANES = 8
HALO = 32
VMEM_LIMIT = 56 * 1024 * 1024

TM_MIX = 256
TM_IN = 512
TM_MOE = 512
TM_COMBINE = 256
DMA_UNROLL = 8
CONV_ROWS = 64


def _params(sem):
    return pltpu.CompilerParams(dimension_semantics=sem, vmem_limit_bytes=VMEM_LIMIT)


def _resident(shape, index_map):
    return pl.BlockSpec(shape, index_map, pipeline_mode=pl.Buffered(1))


def _tile(n, want):
    t = min(want, n)
    while n % t:
        t -= LANES
    return t


def _rms(x, g):
    return x * lax.rsqrt(jnp.mean(x * x, axis=-1, keepdims=True) + EPS) * g


def _ada_kernel(c_ref, w_ref, b_ref, o_ref):
    c = c_ref[...]
    s = c * jax.nn.sigmoid(c)
    base = jnp.dot(s, w_ref[...], preferred_element_type=F32, precision=lax.Precision.HIGHEST)
    o_ref[...] = base[None, :, :] + b_ref[...][:, None, :]


def _ada(c, ada_w, ada_b, tn=1024):
    bsz, d = c.shape
    depth, n6 = ada_b.shape
    tn = _tile(n6, tn)
    return pl.pallas_call(
        _ada_kernel,
        grid=(n6 // tn,),
        in_specs=[pl.BlockSpec((bsz, d), lambda j: (0, 0)),
                  pl.BlockSpec((d, tn), lambda j: (0, j)),
                  pl.BlockSpec((depth, tn), lambda j: (0, j))],
        out_specs=pl.BlockSpec((depth, bsz, tn), lambda j: (0, 0, j)),
        out_shape=jax.ShapeDtypeStruct((depth, bsz, n6), F32),
        compiler_params=_params(("arbitrary",)),
        name="ada_mod",
    )(c, ada_w, ada_b)


def _norm_mm_kernel(x_ref, g_ref, sc_ref, sh_ref, w_ref, z_ref, *, tn):
    h = _rms(x_ref[...], g_ref[...]) * (1.0 + sc_ref[0]) + sh_ref[0]
    hb = h.astype(BF)
    nout = z_ref.shape[1]
    for n0 in range(0, nout, tn):
        z_ref[:, n0:n0 + tn] = jnp.dot(hb, w_ref[:, n0:n0 + tn], preferred_element_type=F32)


def _norm_mm(x, g, sc, sh, w, layer, seq, tm=TM_IN, tn=512):
    n, d = x.shape
    nout = w.shape[2]
    tm = min(tm, seq)
    bpb = seq // tm
    return pl.pallas_call(
        functools.partial(_norm_mm_kernel, tn=min(tn, nout)),
        grid=(n // tm,),
        in_specs=[pl.BlockSpec((tm, d), lambda i: (i, 0)),
                  _resident((1, d), lambda i: (0, 0)),
                  pl.BlockSpec((1, 1, d), lambda i: (i // bpb, 0, 0)),
                  pl.BlockSpec((1, 1, d), lambda i: (i // bpb, 0, 0)),
                  _resident((None, d, nout), lambda i: (layer, 0, 0))],
        out_specs=pl.BlockSpec((tm, nout), lambda i: (i, 0)),
        out_shape=jax.ShapeDtypeStruct((n, nout), F32),
        compiler_params=_params(("arbitrary",)),
        name="norm_in_proj",
    )(x, g.reshape(1, d), sc, sh, w)


def _out_proj_residual(ycat, x_ref, g1_ref, wout_ref, xo_ref, tn=512):
    d = xo_ref.shape[1]
    yc = ycat[...]
    for n0 in range(0, d, tn):
        y = jnp.dot(yc, wout_ref[:, n0:n0 + tn], preferred_element_type=F32)
        xo_ref[:, n0:n0 + tn] = x_ref[:, n0:n0 + tn] + g1_ref[0][:, n0:n0 + tn] * y


def _ffn_input(xo_ref, fg_ref, sc2_ref, sh2_ref):
    return _rms(xo_ref[...], fg_ref[...]) * (1.0 + sc2_ref[0]) + sh2_ref[0]


def _dwconv_lanes(gbuf, cvw_ref, cvb_ref, conv, cb, tm):
    lanes = slice(cb * LANES, (cb + 1) * LANES)
    off = HALO - (CONV_B - 1)
    rc = min(CONV_ROWS, tm)
    win = rc + HALO + SUBLANES
    wk = [cvw_ref[k:k + 1, lanes] for k in range(CONV_B)]
    bias = cvb_ref[:, lanes]
    for r0 in range(0, tm, rc):
        col = gbuf[r0:r0 + win, lanes]
        acc = None
        for r in range(SUBLANES):
            s = col if r == 0 else pltpu.roll(col, win - r, axis=0)
            for k in range(CONV_B):
                if (off + k) % SUBLANES != r:
                    continue
                q = (off + k) // SUBLANES
                t = wk[k] * s[SUBLANES * q:SUBLANES * q + rc, :]
                acc = t if acc is None else acc + t
        conv[r0:r0 + rc, lanes] = acc + bias


def _even_mix_kernel(z_ref, halo_ref, x_ref, g1_ref, wout_ref, ws_ref, gmb_ref, gng_ref,
                     cvw_ref, cvb_ref, lng_ref, lnb_ref, fg_ref, sc2_ref, sh2_ref,
                     xo_ref, h2_ref, gbuf, conv, ycat, *, tm, da, db, bpb):
    i = pl.program_id(0)
    dg = da // N_GROUPS_A

    r = lax.broadcasted_iota(jnp.int32, (GMLP_BLOCK, GMLP_BLOCK), 0) // CHUNK
    c = lax.broadcasted_iota(jnp.int32, (GMLP_BLOCK, GMLP_BLOCK), 1) // CHUNK
    causal = c <= r
    ws = [jnp.where(causal, ws_ref[g], 0.0).astype(BF) for g in range(N_GROUPS_A)]
    for n in range(tm // GMLP_BLOCK):
        rows = slice(n * GMLP_BLOCK, (n + 1) * GMLP_BLOCK)
        v = jax.nn.gelu(z_ref[rows, da:2 * da], approximate=True)
        vb = _rms(v, gng_ref[...]).astype(BF)
        for g in range(N_GROUPS_A):
            cols = slice(g * dg, (g + 1) * dg)
            sp = jnp.dot(ws[g], vb[:, cols], preferred_element_type=F32) + gmb_ref[:, g:g + 1]
            u = jax.nn.gelu(z_ref[rows, cols], approximate=True)
            ycat[rows, cols] = (u * sp).astype(BF)

    a = z_ref[:, 2 * da:2 * da + db]
    gate = z_ref[:, 2 * da + db:]
    gbuf[HALO:HALO + tm, :] = a * jax.nn.sigmoid(gate)
    hglu = halo_ref[:, 0:db] * jax.nn.sigmoid(halo_ref[:, db:2 * db])
    gbuf[0:HALO, :] = jnp.where(i % bpb == 0, 0.0, hglu)

    gbuf[HALO + tm:, :] = jnp.zeros((SUBLANES, db), F32)
    for cb in range(db // LANES):
        _dwconv_lanes(gbuf, cvw_ref, cvb_ref, conv, cb, tm)

    cv = conv[...]
    mu = jnp.mean(cv, axis=-1, keepdims=True)
    xc = cv - mu
    ln = xc * lax.rsqrt(jnp.mean(xc * xc, axis=-1, keepdims=True) + EPS) * lng_ref[...] + lnb_ref[...]
    ycat[:, da:da + db] = (ln * jax.nn.sigmoid(ln)).astype(BF)

    _out_proj_residual(ycat, x_ref, g1_ref, wout_ref, xo_ref)
    h2_ref[...] = _ffn_input(xo_ref, fg_ref, sc2_ref, sh2_ref).astype(h2_ref.dtype)


def _even_mix(z, x, g1, wout, layer, gm_ws, gm_b, gm_norm_g, cv_w, cv_b, cv_ln_g, cv_ln_b,
              ffn_g, sc2, sh2, seq, tm=TM_MIX):
    n, d = x.shape
    da = gm_norm_g.shape[0]
    db = cv_b.shape[0]
    tm = min(tm, seq)
    bpb = seq // tm
    hb = tm // HALO
    vec = lambda k: _resident((1, k), lambda i: (0, 0))
    per_batch = pl.BlockSpec((1, 1, d), lambda i: (i // bpb, 0, 0))
    return pl.pallas_call(
        functools.partial(_even_mix_kernel, tm=tm, da=da, db=db, bpb=bpb),
        grid=(n // tm,),
        in_specs=[pl.BlockSpec((tm, 2 * da + 2 * db), lambda i: (i, 0)),
                  pl.BlockSpec((HALO, 2 * db), lambda i: (jnp.maximum(i * hb - 1, 0), da // db)),
                  pl.BlockSpec((tm, d), lambda i: (i, 0)),
                  per_batch,
                  _resident((None, da + db, d), lambda i: (layer, 0, 0)),
                  _resident((N_GROUPS_A, GMLP_BLOCK, GMLP_BLOCK), lambda i: (0, 0, 0)),
                  _resident((GMLP_BLOCK, N_GROUPS_A), lambda i: (0, 0)),
                  vec(da),
                  _resident((CONV_B, db), lambda i: (0, 0)),
                  vec(db), vec(db), vec(db), vec(d),
                  per_batch, per_batch],
        out_specs=[pl.BlockSpec((tm, d), lambda i: (i, 0)),
                   pl.BlockSpec((tm, d), lambda i: (i, 0))],
        out_shape=[jax.ShapeDtypeStruct((n, d), F32),
                   jax.ShapeDtypeStruct((n, d), BF)],
        scratch_shapes=[pltpu.VMEM((HALO + tm + SUBLANES, db), F32),
                        pltpu.VMEM((tm, db), F32),
                        pltpu.VMEM((tm, da + db), BF)],
        compiler_params=_params(("arbitrary",)),
        name="even_mixer_out_proj",
    )(z, z, x, g1, wout, gm_ws, gm_b.T, gm_norm_g.reshape(1, da), cv_w, cv_b.reshape(1, db),
      cv_ln_g.reshape(1, db), cv_ln_b.reshape(1, db), ffn_g.reshape(1, d), sc2, sh2)


def _odd_mix_kernel(z_ref, halo_ref, x_ref, g1_ref, wout_ref, pw_ref, ps_ref, scw_ref, scb_ref,
                    fg_ref, sc2_ref, sh2_ref, rw_ref,
                    xo_ref, h2_ref, ti_ref, tg_ref, pbuf, qbuf, ycat, *, tm, dc, dd, bpb):
    i = pl.program_id(0)
    first = i % bpb == 0
    dgc = dc // len(POOL_WINDOWS)

    pbuf[HALO:HALO + tm, :] = z_ref[:, 0:dc]
    pbuf[0:HALO, :] = jnp.where(first, 0.0, halo_ref[:, 0:dc])
    pos = (i % bpb) * tm + lax.broadcasted_iota(jnp.int32, (tm, 1), 0)
    for g, w in enumerate(POOL_WINDOWS):
        cols = slice(g * dgc, (g + 1) * dgc)
        tok = pbuf[HALO:HALO + tm, cols]
        acc = tok
        for dlt in range(1, w):
            acc = acc + pbuf[HALO - dlt:HALO - dlt + tm, cols]
        cnt = jnp.minimum(pos + 1, w).astype(F32)
        diff = (acc / cnt - tok).astype(BF)
        mixed = jnp.dot(diff, pw_ref[g], preferred_element_type=F32) * ps_ref[:, cols]
        ycat[:, cols] = mixed.astype(BF)

    qbuf[HALO:HALO + tm, :] = z_ref[:, dc + dd:dc + 2 * dd] * z_ref[:, dc + 2 * dd:]
    hp = halo_ref[:, dc + dd:dc + 2 * dd] * halo_ref[:, dc + 2 * dd:]
    qbuf[0:HALO, :] = jnp.where(first, 0.0, hp)
    off = HALO - (CONV_D - 1)
    y = scw_ref[0:1, :] * qbuf[off:off + tm, :]
    for k in range(1, CONV_D):
        y = y + scw_ref[k:k + 1, :] * qbuf[off + k:off + k + tm, :]
    y = y + scb_ref[...]
    ycat[:, dc:dc + dd] = (z_ref[:, dc:dc + dd] * y).astype(BF)

    _out_proj_residual(ycat, x_ref, g1_ref, wout_ref, xo_ref)
    h2 = _ffn_input(xo_ref, fg_ref, sc2_ref, sh2_ref)
    h2_ref[...] = h2

    hh = h2.astype(BF)
    hl = (h2 - hh.astype(F32)).astype(BF)
    p = jnp.dot(hh, rw_ref[...], preferred_element_type=F32)
    q = jnp.dot(hl, rw_ref[:, 0:LANES], preferred_element_type=F32)
    logits = p[:, 0:LANES] + p[:, LANES:] + q

    lane = lax.broadcasted_iota(jnp.int32, logits.shape, 1)
    lane_f = lane.astype(F32)
    neg = jnp.float32(-jnp.inf)
    l1 = jnp.where(lane < N_EXPERTS, logits, neg)
    m1 = jnp.max(l1, axis=-1, keepdims=True)
    i1 = jnp.min(jnp.where(l1 == m1, lane_f, float(LANES)), axis=-1, keepdims=True)
    l2 = jnp.where(lane_f == i1, neg, l1)
    m2 = jnp.max(l2, axis=-1, keepdims=True)
    i2 = jnp.min(jnp.where(l2 == m2, lane_f, float(LANES)), axis=-1, keepdims=True)
    e = jnp.exp(m2 - m1)
    ga = 1.0 / (1.0 + e)
    gb = e / (1.0 + e)
    ti_ref[...] = jnp.where(lane == 0, i1, jnp.where(lane == 1, i2, 0.0)).astype(jnp.int32)
    tg_ref[...] = jnp.where(lane == 0, ga, jnp.where(lane == 1, gb, 0.0))


def _odd_mix(z, x, g1, wout, layer, pool_w, pool_scale, sc_w, sc_b, ffn_g, sc2, sh2, router_w, seq, tm=TM_MIX):
    n, d = x.shape
    dc = pool_scale.shape[0]
    dd = sc_b.shape[0]
    ng = len(POOL_WINDOWS)
    dgc = dc // ng
    tm = min(tm, seq)
    bpb = seq // tm
    hb = tm // HALO
    rw_hi = router_w.astype(BF)
    rw_lo = (router_w - rw_hi.astype(F32)).astype(BF)
    rw = jnp.zeros((d, 2 * LANES), BF).at[:, :N_EXPERTS].set(rw_hi).at[:, LANES:LANES + N_EXPERTS].set(rw_lo)
    vec = lambda k: _resident((1, k), lambda i: (0, 0))
    per_batch = pl.BlockSpec((1, 1, d), lambda i: (i // bpb, 0, 0))
    return pl.pallas_call(
        functools.partial(_odd_mix_kernel, tm=tm, dc=dc, dd=dd, bpb=bpb),
        grid=(n // tm,),
        in_specs=[pl.BlockSpec((tm, dc + 3 * dd), lambda i: (i, 0)),
                  pl.BlockSpec((HALO, dc + 3 * dd), lambda i: (jnp.maximum(i * hb - 1, 0), 0)),
                  pl.BlockSpec((tm, d), lambda i: (i, 0)),
                  per_batch,
                  _resident((None, dc + dd, d), lambda i: (layer, 0, 0)),
                  _resident((ng, dgc, dgc), lambda i: (0, 0, 0)),
                  vec(dc),
                  _resident((CONV_D, dd), lambda i: (0, 0)),
                  vec(dd), vec(d),
                  per_batch, per_batch,
                  _resident((d, 2 * LANES), lambda i: (0, 0))],
        out_specs=[pl.BlockSpec((tm, d), lambda i: (i, 0)),
                   pl.BlockSpec((tm, d), lambda i: (i, 0)),
                   pl.BlockSpec((tm, LANES), lambda i: (i, 0)),
                   pl.BlockSpec((tm, LANES), lambda i: (i, 0))],
        out_shape=[jax.ShapeDtypeStruct((n, d), F32),
                   jax.ShapeDtypeStruct((n, d), F32),
                   jax.ShapeDtypeStruct((n, LANES), jnp.int32),
                   jax.ShapeDtypeStruct((n, LANES), F32)],
        scratch_shapes=[pltpu.VMEM((HALO + tm, dc), F32),
                        pltpu.VMEM((HALO + tm, dd), F32),
                        pltpu.VMEM((tm, dc + dd), BF)],
        compiler_params=_params(("arbitrary",)),
        name="odd_mixer_out_proj_router",
    )(z, z, x, g1, wout, pool_w, pool_scale.reshape(1, dc), sc_w, sc_b.reshape(1, dd),
      ffn_g.reshape(1, d), sc2, sh2, rw)


def _swiglu_up_kernel(h_ref, w1_ref, w3_ref, o_ref, w1b, w3b, *, rc):
    @pl.when(pl.program_id(1) == 0)
    def _():
        w1b[...] = w1_ref[...].astype(BF)
        w3b[...] = w3_ref[...].astype(BF)

    for r0 in range(0, h_ref.shape[0], rc):
        h = h_ref[r0:r0 + rc, :]
        a = jnp.dot(h, w1b[...], preferred_element_type=F32)
        b = jnp.dot(h, w3b[...], preferred_element_type=F32)
        o_ref[r0:r0 + rc, :] = (a * jax.nn.sigmoid(a) * b).astype(BF)


def _swiglu_up(h, w1, w3, layer, tm=2048, tn=512):
    n, d = h.shape
    f = w1.shape[2]
    tm = min(tm, n)
    tn = _tile(f, tn)
    wspec = pl.BlockSpec((None, d, tn), lambda j, i: (layer, 0, j))
    return pl.pallas_call(
        functools.partial(_swiglu_up_kernel, rc=min(512, tm)),
        grid=(f // tn, n // tm),
        in_specs=[pl.BlockSpec((tm, d), lambda j, i: (i, 0)), wspec, wspec],
        out_specs=pl.BlockSpec((tm, tn), lambda j, i: (i, j)),
        out_shape=jax.ShapeDtypeStruct((n, f), BF),
        scratch_shapes=[pltpu.VMEM((d, tn), BF), pltpu.VMEM((d, tn), BF)],
        compiler_params=_params(("arbitrary", "arbitrary")),
        name="dense_swiglu_up",
    )(h, w1, w3)


def _down_residual_kernel(a_ref, w_ref, x_ref, g_ref, o_ref):
    y = jnp.dot(a_ref[...], w_ref[...], preferred_element_type=F32)
    o_ref[...] = x_ref[...] + g_ref[0] * y


def _down_residual(a, w, layer, x, gate, seq, tm=1024, tn=512):
    n, k = a.shape
    d = w.shape[2]
    tm = min(tm, seq)
    tn = min(tn, d)
    bpb = seq // tm
    return pl.pallas_call(
        _down_residual_kernel,
        grid=(d // tn, n // tm),
        in_specs=[pl.BlockSpec((tm, k), lambda j, i: (i, 0)),
                  pl.BlockSpec((None, k, tn), lambda j, i: (layer, 0, j)),
                  pl.BlockSpec((tm, tn), lambda j, i: (i, j)),
                  pl.BlockSpec((1, 1, tn), lambda j, i: (i // bpb, 0, j))],
        out_specs=pl.BlockSpec((tm, tn), lambda j, i: (i, j)),
        out_shape=jax.ShapeDtypeStruct((n, d), F32),
        compiler_params=_params(("arbitrary", "arbitrary")),
        name="dense_down_residual",
    )(a, w, x, gate)


def _row_copy(src, dst, sem, src_row, dst_row):
    return pltpu.make_async_copy(src.at[pl.ds(src_row, 1), :], dst.at[pl.ds(dst_row, 1), :], sem)


def _gather_kernel(tok_ref, base_ref, lim_ref, used_ref, h_hbm, o_ref, buf, sem, *, tm):
    i = pl.program_id(0)
    used = used_ref[0]

    def issue(blk, slot):
        base = base_ref[blk]
        lim = lim_ref[blk]

        def body(r, carry):
            tok = tok_ref[jnp.minimum(base + r, lim)]
            _row_copy(h_hbm, buf.at[slot], sem.at[slot], tok, r).start()
            return carry

        lax.fori_loop(0, tm, body, 0, unroll=DMA_UNROLL)

    @pl.when(i == 0)
    def _():
        issue(0, 0)

    @pl.when(i + 1 < used)
    def _():
        issue(i + 1, (i + 1) % 2)

    @pl.when(i < used)
    def _():
        slot = i % 2
        pltpu.make_async_copy(h_hbm.at[pl.ds(0, tm), :], buf.at[slot], sem.at[slot]).wait()
        o_ref[...] = buf[slot].astype(BF)

    @pl.when(i >= used)
    def _():
        o_ref[...] = jnp.zeros_like(o_ref)


def _moe_gather(plan, h, tm):
    rows = plan["nb"] * tm
    d = h.shape[1]
    return pl.pallas_call(
        functools.partial(_gather_kernel, tm=tm),
        grid_spec=pltpu.PrefetchScalarGridSpec(
            num_scalar_prefetch=4,
            grid=(plan["nb"],),
            in_specs=[pl.BlockSpec(memory_space=pl.ANY)],
            out_specs=pl.BlockSpec((tm, d), lambda i, *_: (i, 0)),
            scratch_shapes=[pltpu.VMEM((2, tm, d), F32), pltpu.SemaphoreType.DMA((2,))]),
        out_shape=jax.ShapeDtypeStruct((rows, d), BF),
        compiler_params=_params(("arbitrary",)),
        name="moe_gather",
    )(plan["tok_sorted"], plan["base"], plan["lim"], plan["used"], h)


def _moe_up_kernel(be_ref, first_ref, used_ref, x_ref, w1_ref, w3_ref, o_ref, w1b, w3b):
    i = pl.program_id(1)

    @pl.when(first_ref[i] == 1)
    def _():
        w1b[...] = w1_ref[...].astype(BF)
        w3b[...] = w3_ref[...].astype(BF)

    @pl.when(i < used_ref[0])
    def _():
        x = x_ref[...]
        a = jnp.dot(x, w1b[...], preferred_element_type=F32)
        b = jnp.dot(x, w3b[...], preferred_element_type=F32)
        o_ref[...] = (a * jax.nn.sigmoid(a) * b).astype(BF)

    @pl.when(i >= used_ref[0])
    def _():
        o_ref[...] = jnp.zeros_like(o_ref)


def _moe_up(plan, xs, w1, w3, layer, tm, tn=1024):
    rows, d = xs.shape
    f = w1.shape[3]
    tn = _tile(f, tn)
    last = lambda i, used: jnp.minimum(i, used[0] - 1)
    wspec = pl.BlockSpec((None, None, d, tn), lambda j, i, be, first, used: (layer, be[last(i, used)], 0, j))
    return pl.pallas_call(
        _moe_up_kernel,
        grid_spec=pltpu.PrefetchScalarGridSpec(
            num_scalar_prefetch=3,
            grid=(f // tn, rows // tm),
            in_specs=[pl.BlockSpec((tm, d), lambda j, i, be, first, used: (last(i, used), 0)), wspec, wspec],
            out_specs=pl.BlockSpec((tm, tn), lambda j, i, be, first, used: (i, j)),
            scratch_shapes=[pltpu.VMEM((d, tn), BF), pltpu.VMEM((d, tn), BF)]),
        out_shape=jax.ShapeDtypeStruct((rows, f), BF),
        compiler_params=_params(("arbitrary", "arbitrary")),
        name="moe_swiglu_up",
    )(plan["blk_expert"], plan["first"], plan["used"], xs, w1, w3)


def _moe_down_kernel(be_ref, first_ref, used_ref, h_ref, w_ref, o_ref, wb):
    i = pl.program_id(1)

    @pl.when(first_ref[i] == 1)
    def _():
        wb[...] = w_ref[...].astype(BF)

    @pl.when(i < used_ref[0])
    def _():
        o_ref[...] = jnp.dot(h_ref[...], wb[...], preferred_element_type=F32)

    @pl.when(i >= used_ref[0])
    def _():
        o_ref[...] = jnp.zeros_like(o_ref)


def _moe_down(plan, hs, w2, layer, tm, tn=512):
    rows, f = hs.shape
    d = w2.shape[3]
    tn = min(tn, d)
    last = lambda i, used: jnp.minimum(i, used[0] - 1)
    return pl.pallas_call(
        _moe_down_kernel,
        grid_spec=pltpu.PrefetchScalarGridSpec(
            num_scalar_prefetch=3,
            grid=(d // tn, rows // tm),
            in_specs=[pl.BlockSpec((tm, f), lambda j, i, be, first, used: (last(i, used), 0)),
                      pl.BlockSpec((None, None, f, tn),
                                   lambda j, i, be, first, used: (layer, be[last(i, used)], 0, j))],
            out_specs=pl.BlockSpec((tm, tn), lambda j, i, be, first, used: (i, j)),
            scratch_shapes=[pltpu.VMEM((f, tn), BF)]),
        out_shape=jax.ShapeDtypeStruct((rows, d), F32),
        compiler_params=_params(("arbitrary", "arbitrary")),
        name="moe_down",
    )(plan["blk_expert"], plan["first"], plan["used"], hs, w2)


def _combine_kernel(pos_ref, ys_hbm, x_ref, tg_ref, g2_ref, fin_ref, o_ref, buf, sem, *, tm, final):
    i = pl.program_id(0)
    steps = pl.num_programs(0)

    def issue(tile, slot):
        def body(r, carry):
            for k in range(TOP_K):
                src = pos_ref[(tile * tm + r) * TOP_K + k]
                _row_copy(ys_hbm, buf.at[slot], sem.at[slot], src, k * tm + r).start()
            return carry

        lax.fori_loop(0, tm, body, 0, unroll=DMA_UNROLL)

    @pl.when(i == 0)
    def _():
        issue(0, 0)

    @pl.when(i + 1 < steps)
    def _():
        issue(i + 1, (i + 1) % 2)

    slot = i % 2
    pltpu.make_async_copy(ys_hbm.at[pl.ds(0, TOP_K * tm), :], buf.at[slot], sem.at[slot]).wait()
    y = tg_ref[:, 0:1] * buf[slot, 0:tm, :]
    for k in range(1, TOP_K):
        y = y + tg_ref[:, k:k + 1] * buf[slot, k * tm:(k + 1) * tm, :]
    res = x_ref[...] + g2_ref[0] * y
    o_ref[...] = _rms(res, fin_ref[...]) if final else res


def _moe_combine(pos, ys, x, tg, g2, final_g, final, seq, tm=TM_COMBINE):
    n, d = x.shape
    tm = min(tm, seq)
    bpb = seq // tm
    return pl.pallas_call(
        functools.partial(_combine_kernel, tm=tm, final=final),
        grid_spec=pltpu.PrefetchScalarGridSpec(
            num_scalar_prefetch=1,
            grid=(n // tm,),
            in_specs=[pl.BlockSpec(memory_space=pl.ANY),
                      pl.BlockSpec((tm, d), lambda i, pos: (i, 0)),
                      pl.BlockSpec((tm, LANES), lambda i, pos: (i, 0)),
                      pl.BlockSpec((1, 1, d), lambda i, pos: (i // bpb, 0, 0)),
                      _resident((1, d), lambda i, pos: (0, 0))],
            out_specs=pl.BlockSpec((tm, d), lambda i, pos: (i, 0)),
            scratch_shapes=[pltpu.VMEM((2, TOP_K * tm, d), F32), pltpu.SemaphoreType.DMA((2,))]),
        out_shape=jax.ShapeDtypeStruct((n, d), F32),
        compiler_params=_params(("arbitrary",)),
        name="moe_combine",
    )(pos, ys, x, tg, g2, final_g.reshape(1, d))


def _dispatch_plan(top_idx, tm):
    n = top_idx.shape[0]
    pairs = n * TOP_K
    e_flat = top_idx.reshape(-1)
    eids = jnp.arange(N_EXPERTS, dtype=jnp.int32)
    onehot = (e_flat[:, None] == eids[None, :]).astype(jnp.int32)
    csum = jnp.cumsum(onehot, axis=0)
    counts = csum[-1]
    padded = (counts + tm - 1) // tm * tm
    pend = jnp.cumsum(padded)
    pstart = pend - padded
    start = jnp.cumsum(counts) - counts
    pos = jnp.sum(onehot * (csum - 1 + pstart[None, :]), axis=1).astype(jnp.int32)
    order = jnp.argsort(e_flat, stable=True)
    tok_sorted = (order // TOP_K).astype(jnp.int32)
    nb = pairs // tm + N_EXPERTS
    used = (pend[-1] // tm).astype(jnp.int32)
    blk_start = jnp.arange(nb, dtype=jnp.int32) * tm
    be = jnp.minimum(jnp.sum(blk_start[:, None] >= pend[None, :], axis=1), N_EXPERTS - 1)
    be_last = jnp.minimum(jnp.sum((used - 1) * tm >= pend), N_EXPERTS - 1)
    be = jnp.where(jnp.arange(nb) < used, be, be_last).astype(jnp.int32)
    sel = (be[:, None] == eids[None, :]).astype(jnp.int32)
    start_b = jnp.sum(sel * start[None, :], axis=1)
    base = (start_b + blk_start - jnp.sum(sel * pstart[None, :], axis=1)).astype(jnp.int32)
    lim = (start_b + jnp.sum(sel * counts[None, :], axis=1) - 1).astype(jnp.int32)
    first = jnp.concatenate([jnp.ones((1,), jnp.int32), (be[1:] != be[:-1]).astype(jnp.int32)])
    return dict(pos=pos, tok_sorted=tok_sorted, blk_expert=be, first=first, base=base, lim=lim,
                used=used.reshape(1), nb=nb)


def _moe(h2, ti, tg, x, g2, w1, w3, w2, layer, final_g, final, seq):
    plan = _dispatch_plan(ti[:, :TOP_K], TM_MOE)
    xs = _moe_gather(plan, h2, TM_MOE)
    hs = _moe_up(plan, xs, w1, w3, layer, TM_MOE)
    ys = _moe_down(plan, hs, w2, layer, TM_MOE)
    return _moe_combine(plan["pos"], ys, x, tg, g2, final_g, final, seq)


def _final_norm_kernel(x_ref, g_ref, o_ref):
    o_ref[...] = _rms(x_ref[...], g_ref[...])


def _final_norm(x, g, tm=512):
    n, d = x.shape
    tm = min(tm, n)
    return pl.pallas_call(
        _final_norm_kernel,
        grid=(n // tm,),
        in_specs=[pl.BlockSpec((tm, d), lambda i: (i, 0)), _resident((1, d), lambda i: (0, 0))],
        out_specs=pl.BlockSpec((tm, d), lambda i: (i, 0)),
        out_shape=jax.ShapeDtypeStruct((n, d), F32),
        compiler_params=_params(("arbitrary",)),
        name="final_norm",
    )(x, g.reshape(1, d))


def kernel(x, c, ada_w, ada_b, norm_mix_g, norm_ffn_g, final_g, ev_w_in, ev_w_out, gm_ws, gm_b, gm_norm_g, cv_w, cv_b, cv_ln_g, cv_ln_b, ffn_w1, ffn_w3, ffn_w2, od_w_in, od_w_out, pool_w, pool_scale, sc_w, sc_b, router_w, moe_w1, moe_w3, moe_w2):
    bsz, seq, d = x.shape
    depth = ada_b.shape[0]
    xf = x.reshape(bsz * seq, d)
    mod = _ada(c, ada_w, ada_b)
    ev_w_in_b, ev_w_out_b = ev_w_in.astype(BF), ev_w_out.astype(BF)
    od_w_in_b, od_w_out_b = od_w_in.astype(BF), od_w_out.astype(BF)
    ffn_w2_b, pool_w_b = ffn_w2.astype(BF), pool_w.astype(BF)
    for l in range(depth):
        m = mod[l].reshape(bsz, 6, 1, d)
        sh1, sc1, g1, sh2, sc2, g2 = [m[:, k] for k in range(6)]
        i = l // 2
        last = l == depth - 1
        if l % 2 == 0:
            z = _norm_mm(xf, norm_mix_g[l], sc1, sh1, ev_w_in_b, i, seq)
            xf, h2 = _even_mix(z, xf, g1, ev_w_out_b, i, gm_ws[i], gm_b[i], gm_norm_g[i],
                               cv_w[i], cv_b[i], cv_ln_g[i], cv_ln_b[i], norm_ffn_g[l], sc2, sh2, seq)
            hs = _swiglu_up(h2, ffn_w1, ffn_w3, i)
            xf = _down_residual(hs, ffn_w2_b, i, xf, g2, seq)
            if last:
                xf = _final_norm(xf, final_g)
        else:
            z = _norm_mm(xf, norm_mix_g[l], sc1, sh1, od_w_in_b, i, seq)
            xf, h2, ti, tg = _odd_mix(z, xf, g1, od_w_out_b, i, pool_w_b[i], pool_scale[i],
                                      sc_w[i], sc_b[i], norm_ffn_g[l], sc2, sh2, router_w[i], seq)
            xf = _moe(h2, ti, tg, xf, g2, moe_w1, moe_w3, moe_w2, i, final_g, last, seq)
    return xf.reshape(bsz, seq, d)
```

```python
import functools

import jax
import jax.numpy as jnp
from jax import lax
from jax.experimental import pallas as pl
from jax.experimental.pallas import tpu as pltpu

BF = jnp.bfloat16
F32 = jnp.float32
EPS = 1e-6

CHUNK = 64
GMLP_BLOCK = 128
N_GROUPS_A = 4
CONV_B = 31
CONV_D = 3
POOL_WINDOWS = (2, 4, 8, 16)
N_EXPERTS = 8
TOP_K = 2

LANES = 128
SUBLANES = 8
HALO = 32
VMEM_LIMIT = 60 * 1024 * 1024

TM_MIX = 256
TM_IN = 512
TM_MOE = 512
TM_COMBINE = 256
DMA_UNROLL = 8
CONV_ROWS = 64


def _params(sem):
    return pltpu.CompilerParams(dimension_semantics=sem, vmem_limit_bytes=VMEM_LIMIT)


def _resident(shape, index_map):
    return pl.BlockSpec(shape, index_map, pipeline_mode=pl.Buffered(1))


def _tile(n, want):
    t = min(want, n)
    while n % t:
        t -= LANES
    return t


def _rms(x, g):
    return x * lax.rsqrt(jnp.mean(x * x, axis=-1, keepdims=True) + EPS) * g


def _ada_kernel(c_ref, w_ref, b_ref, o_ref):
    c = c_ref[...]
    s = c * jax.nn.sigmoid(c)
    base = jnp.dot(s, w_ref[...], preferred_element_type=F32, precision=lax.Precision.HIGHEST)
    o_ref[...] = base[None, :, :] + b_ref[...][:, None, :]


def _ada(c, ada_w, ada_b, tn=1024):
    bsz, d = c.shape
    depth, n6 = ada_b.shape
    tn = _tile(n6, tn)
    return pl.pallas_call(
        _ada_kernel,
        grid=(n6 // tn,),
        in_specs=[pl.BlockSpec((bsz, d), lambda j: (0, 0)),
                  pl.BlockSpec((d, tn), lambda j: (0, j)),
                  pl.BlockSpec((depth, tn), lambda j: (0, j))],
        out_specs=pl.BlockSpec((depth, bsz, tn), lambda j: (0, 0, j)),
        out_shape=jax.ShapeDtypeStruct((depth, bsz, n6), F32),
        compiler_params=_params(("arbitrary",)),
        name="ada_mod",
    )(c, ada_w, ada_b)


def _norm_mm_kernel(x_ref, g_ref, sc_ref, sh_ref, w_ref, z_ref, *, tn):
    h = _rms(x_ref[...], g_ref[...]) * (1.0 + sc_ref[0]) + sh_ref[0]
    hb = h.astype(BF)
    nout = z_ref.shape[1]
    for n0 in range(0, nout, tn):
        z_ref[:, n0:n0 + tn] = jnp.dot(hb, w_ref[:, n0:n0 + tn], preferred_element_type=F32)


def _norm_mm(x, g, sc, sh, w, layer, seq, tm=TM_IN, tn=512):
    n, d = x.shape
    nout = w.shape[2]
    tm = min(tm, seq)
    bpb = seq // tm
    return pl.pallas_call(
        functools.partial(_norm_mm_kernel, tn=min(tn, nout)),
        grid=(n // tm,),
        in_specs=[pl.BlockSpec((tm, d), lambda i: (i, 0)),
                  _resident((1, d), lambda i: (0, 0)),
                  pl.BlockSpec((1, 1, d), lambda i: (i // bpb, 0, 0)),
                  pl.BlockSpec((1, 1, d), lambda i: (i // bpb, 0, 0)),
                  _resident((None, d, nout), lambda i: (layer, 0, 0))],
        out_specs=pl.BlockSpec((tm, nout), lambda i: (i, 0)),
        out_shape=jax.ShapeDtypeStruct((n, nout), F32),
        compiler_params=_params(("arbitrary",)),
        name="norm_in_proj",
    )(x, g.reshape(1, d), sc, sh, w)


def _out_proj_residual(ycat, x_ref, g1_ref, wout_ref, xo_ref, tn=512):
    d = xo_ref.shape[1]
    yc = ycat[...]
    for n0 in range(0, d, tn):
        y = jnp.dot(yc, wout_ref[:, n0:n0 + tn], preferred_element_type=F32)
        xo_ref[:, n0:n0 + tn] = x_ref[:, n0:n0 + tn] + g1_ref[0][:, n0:n0 + tn] * y


def _ffn_input(xo_ref, fg_ref, sc2_ref, sh2_ref):
    return _rms(xo_ref[...], fg_ref[...]) * (1.0 + sc2_ref[0]) + sh2_ref[0]


def _dwconv_lanes(gbuf, cvw_ref, cvb_ref, conv, cb, tm):
    lanes = slice(cb * LANES, (cb + 1) * LANES)
    off = HALO - (CONV_B - 1)
    rc = min(CONV_ROWS, tm)
    win = rc + HALO + SUBLANES
    wk = [cvw_ref[k:k + 1, lanes] for k in range(CONV_B)]
    bias = cvb_ref[:, lanes]
    for r0 in range(0, tm, rc):
        col = gbuf[r0:r0 + win, lanes]
        acc = None
        for r in range(SUBLANES):
            s = col if r == 0 else pltpu.roll(col, win - r, axis=0)
            for k in range(CONV_B):
                if (off + k) % SUBLANES != r:
                    continue
                q = (off + k) // SUBLANES
                t = wk[k] * s[SUBLANES * q:SUBLANES * q + rc, :]
                acc = t if acc is None else acc + t
        conv[r0:r0 + rc, lanes] = acc + bias


def _even_mix_kernel(z_ref, halo_ref, x_ref, g1_ref, wout_ref, ws_ref, gmb_ref, gng_ref,
                     cvw_ref, cvb_ref, lng_ref, lnb_ref, fg_ref, sc2_ref, sh2_ref,
                     xo_ref, h2_ref, gbuf, conv, ycat, *, tm, da, db, bpb):
    i = pl.program_id(0)
    dg = da // N_GROUPS_A

    r = lax.broadcasted_iota(jnp.int32, (GMLP_BLOCK, GMLP_BLOCK), 0) // CHUNK
    c = lax.broadcasted_iota(jnp.int32, (GMLP_BLOCK, GMLP_BLOCK), 1) // CHUNK
    causal = c <= r
    ws = [jnp.where(causal, ws_ref[g], 0.0).astype(BF) for g in range(N_GROUPS_A)]
    for n in range(tm // GMLP_BLOCK):
        rows = slice(n * GMLP_BLOCK, (n + 1) * GMLP_BLOCK)
        v = jax.nn.gelu(z_ref[rows, da:2 * da], approximate=True)
        vb = _rms(v, gng_ref[...]).astype(BF)
        for g in range(N_GROUPS_A):
            cols = slice(g * dg, (g + 1) * dg)
            sp = jnp.dot(ws[g], vb[:, cols], preferred_element_type=F32) + gmb_ref[:, g:g + 1]
            u = jax.nn.gelu(z_ref[rows, cols], approximate=True)
            ycat[rows, cols] = (u * sp).astype(BF)

    a = z_ref[:, 2 * da:2 * da + db]
    gate = z_ref[:, 2 * da + db:]
    gbuf[HALO:HALO + tm, :] = a * jax.nn.sigmoid(gate)
    hglu = halo_ref[:, 0:db] * jax.nn.sigmoid(halo_ref[:, db:2 * db])
    gbuf[0:HALO, :] = jnp.where(i % bpb == 0, 0.0, hglu)

    gbuf[HALO + tm:, :] = jnp.zeros((SUBLANES, db), F32)
    for cb in range(db // LANES):
        _dwconv_lanes(gbuf, cvw_ref, cvb_ref, conv, cb, tm)

    cv = conv[...]
    mu = jnp.mean(cv, axis=-1, keepdims=True)
    xc = cv - mu
    ln = xc * lax.rsqrt(jnp.mean(xc * xc, axis=-1, keepdims=True) + EPS) * lng_ref[...] + lnb_ref[...]
    ycat[:, da:da + db] = (ln * jax.nn.sigmoid(ln)).astype(BF)

    _out_proj_residual(ycat, x_ref, g1_ref, wout_ref, xo_ref)
    h2_ref[...] = _ffn_input(xo_ref, fg_ref, sc2_ref, sh2_ref).astype(h2_ref.dtype)


def _even_mix(z, x, g1, wout, layer, gm_ws, gm_b, gm_norm_g, cv_w, cv_b, cv_ln_g, cv_ln_b,
              ffn_g, sc2, sh2, seq, tm=TM_MIX):
    n, d = x.shape
    da = gm_norm_g.shape[0]
    db = cv_b.shape[0]
    tm = min(tm, seq)
    bpb = seq // tm
    hb = tm // HALO
    vec = lambda k: _resident((1, k), lambda i: (0, 0))
    per_batch = pl.BlockSpec((1, 1, d), lambda i: (i // bpb, 0, 0))
    return pl.pallas_call(
        functools.partial(_even_mix_kernel, tm=tm, da=da, db=db, bpb=bpb),
        grid=(n // tm,),
        in_specs=[pl.BlockSpec((tm, 2 * da + 2 * db), lambda i: (i, 0)),
                  pl.BlockSpec((HALO, 2 * db), lambda i: (jnp.maximum(i * hb - 1, 0), da // db)),
                  pl.BlockSpec((tm, d), lambda i: (i, 0)),
                  per_batch,
                  _resident((None, da + db, d), lambda i: (layer, 0, 0)),
                  _resident((N_GROUPS_A, GMLP_BLOCK, GMLP_BLOCK), lambda i: (0, 0, 0)),
                  _resident((GMLP_BLOCK, N_GROUPS_A), lambda i: (0, 0)),
                  vec(da),
                  _resident((CONV_B, db), lambda i: (0, 0)),
                  vec(db), vec(db), vec(db), vec(d),
                  per_batch, per_batch],
        out_specs=[pl.BlockSpec((tm, d), lambda i: (i, 0)),
                   pl.BlockSpec((tm, d), lambda i: (i, 0))],
        out_shape=[jax.ShapeDtypeStruct((n, d), F32),
                   jax.ShapeDtypeStruct((n, d), BF)],
        scratch_shapes=[pltpu.VMEM((HALO + tm + SUBLANES, db), F32),
                        pltpu.VMEM((tm, db), F32),
                        pltpu.VMEM((tm, da + db), BF)],
        compiler_params=_params(("arbitrary",)),
        name="even_mixer_out_proj",
    )(z, z, x, g1, wout, gm_ws, gm_b.T, gm_norm_g.reshape(1, da), cv_w, cv_b.reshape(1, db),
      cv_ln_g.reshape(1, db), cv_ln_b.reshape(1, db), ffn_g.reshape(1, d), sc2, sh2)


def _odd_mix_kernel(z_ref, halo_ref, x_ref, g1_ref, wout_ref, pw_ref, ps_ref, scw_ref, scb_ref,
                    fg_ref, sc2_ref, sh2_ref, rw_ref,
                    xo_ref, h2_ref, ti_ref, tg_ref, pbuf, qbuf, ycat, *, tm, dc, dd, bpb):
    i = pl.program_id(0)
    first = i % bpb == 0
    dgc = dc // len(POOL_WINDOWS)

    pbuf[HALO:HALO + tm, :] = z_ref[:, 0:dc]
    pbuf[0:HALO, :] = jnp.where(first, 0.0, halo_ref[:, 0:dc])
    pos = (i % bpb) * tm + lax.broadcasted_iota(jnp.int32, (tm, 1), 0)
    for g, w in enumerate(POOL_WINDOWS):
        cols = slice(g * dgc, (g + 1) * dgc)
        tok = pbuf[HALO:HALO + tm, cols]
        acc = tok
        for dlt in range(1, w):
            acc = acc + pbuf[HALO - dlt:HALO - dlt + tm, cols]
        cnt = jnp.minimum(pos + 1, w).astype(F32)
        diff = (acc / cnt - tok).astype(BF)
        mixed = jnp.dot(diff, pw_ref[g], preferred_element_type=F32) * ps_ref[:, cols]
        ycat[:, cols] = mixed.astype(BF)

    qbuf[HALO:HALO + tm, :] = z_ref[:, dc + dd:dc + 2 * dd] * z_ref[:, dc + 2 * dd:]
    hp = halo_ref[:, dc + dd:dc + 2 * dd] * halo_ref[:, dc + 2 * dd:]
    qbuf[0:HALO, :] = jnp.where(first, 0.0, hp)
    off = HALO - (CONV_D - 1)
    y = scw_ref[0:1, :] * qbuf[off:off + tm, :]
    for k in range(1, CONV_D):
        y = y + scw_ref[k:k + 1, :] * qbuf[off + k:off + k + tm, :]
    y = y + scb_ref[...]
    ycat[:, dc:dc + dd] = (z_ref[:, dc:dc + dd] * y).astype(BF)

    _out_proj_residual(ycat, x_ref, g1_ref, wout_ref, xo_ref)
    h2 = _ffn_input(xo_ref, fg_ref, sc2_ref, sh2_ref)
    h2_ref[...] = h2

    hh = h2.astype(BF)
    hl = (h2 - hh.astype(F32)).astype(BF)
    p = jnp.dot(hh, rw_ref[...], preferred_element_type=F32)
    q = jnp.dot(hl, rw_ref[:, 0:LANES], preferred_element_type=F32)
    logits = p[:, 0:LANES] + p[:, LANES:] + q

    lane = lax.broadcasted_iota(jnp.int32, logits.shape, 1)
    lane_f = lane.astype(F32)
    neg = jnp.float32(-jnp.inf)
    l1 = jnp.where(lane < N_EXPERTS, logits, neg)
    m1 = jnp.max(l1, axis=-1, keepdims=True)
    i1 = jnp.min(jnp.where(l1 == m1, lane_f, float(LANES)), axis=-1, keepdims=True)
    l2 = jnp.where(lane_f == i1, neg, l1)
    m2 = jnp.max(l2, axis=-1, keepdims=True)
    i2 = jnp.min(jnp.where(l2 == m2, lane_f, float(LANES)), axis=-1, keepdims=True)
    e = jnp.exp(m2 - m1)
    ga = 1.0 / (1.0 + e)
    gb = e / (1.0 + e)
    ti_ref[...] = jnp.where(lane == 0, i1, jnp.where(lane == 1, i2, 0.0)).astype(jnp.int32)
    tg_ref[...] = jnp.where(lane == 0, ga, jnp.where(lane == 1, gb, 0.0))


def _odd_mix(z, x, g1, wout, layer, pool_w, pool_scale, sc_w, sc_b, ffn_g, sc2, sh2, router_w, seq, tm=TM_MIX):
    n, d = x.shape
    dc = pool_scale.shape[0]
    dd = sc_b.shape[0]
    ng = len(POOL_WINDOWS)
    dgc = dc // ng
    tm = min(tm, seq)
    bpb = seq // tm
    hb = tm // HALO
    rw_hi = router_w.astype(BF)
    rw_lo = (router_w - rw_hi.astype(F32)).astype(BF)
    rw = jnp.zeros((d, 2 * LANES), BF).at[:, :N_EXPERTS].set(rw_hi).at[:, LANES:LANES + N_EXPERTS].set(rw_lo)
    vec = lambda k: _resident((1, k), lambda i: (0, 0))
    per_batch = pl.BlockSpec((1, 1, d), lambda i: (i // bpb, 0, 0))
    return pl.pallas_call(
        functools.partial(_odd_mix_kernel, tm=tm, dc=dc, dd=dd, bpb=bpb),
        grid=(n // tm,),
        in_specs=[pl.BlockSpec((tm, dc + 3 * dd), lambda i: (i, 0)),
                  pl.BlockSpec((HALO, dc + 3 * dd), lambda i: (jnp.maximum(i * hb - 1, 0), 0)),
                  pl.BlockSpec((tm, d), lambda i: (i, 0)),
                  per_batch,
                  _resident((None, dc + dd, d), lambda i: (layer, 0, 0)),
                  _resident((ng, dgc, dgc), lambda i: (0, 0, 0)),
                  vec(dc),
                  _resident((CONV_D, dd), lambda i: (0, 0)),
                  vec(dd), vec(d),
                  per_batch, per_batch,
                  _resident((d, 2 * LANES), lambda i: (0, 0))],
        out_specs=[pl.BlockSpec((tm, d), lambda i: (i, 0)),
                   pl.BlockSpec((tm, d), lambda i: (i, 0)),
                   pl.BlockSpec((tm, LANES), lambda i: (i, 0)),
                   pl.BlockSpec((tm, LANES), lambda i: (i, 0))],
        out_shape=[jax.ShapeDtypeStruct((n, d), F32),
                   jax.ShapeDtypeStruct((n, d), F32),
                   jax.ShapeDtypeStruct((n, LANES), jnp.int32),
                   jax.ShapeDtypeStruct((n, LANES), F32)],
        scratch_shapes=[pltpu.VMEM((HALO + tm, dc), F32),
                        pltpu.VMEM((HALO + tm, dd), F32),
                        pltpu.VMEM((tm, dc + dd), BF)],
        compiler_params=_params(("arbitrary",)),
        name="odd_mixer_out_proj_router",
    )(z, z, x, g1, wout, pool_w, pool_scale.reshape(1, dc), sc_w, sc_b.reshape(1, dd),
      ffn_g.reshape(1, d), sc2, sh2, rw)


def _swiglu_up_kernel(h_ref, w1_ref, w3_ref, o_ref, w1b, w3b, *, rc):
    @pl.when(pl.program_id(1) == 0)
    def _():
        w1b[...] = w1_ref[...].astype(BF)
        w3b[...] = w3_ref[...].astype(BF)

    for r0 in range(0, h_ref.shape[0], rc):
        h = h_ref[r0:r0 + rc, :]
        a = jnp.dot(h, w1b[...], preferred_element_type=F32)
        b = jnp.dot(h, w3b[...], preferred_element_type=F32)
        o_ref[r0:r0 + rc, :] = (a * jax.nn.sigmoid(a) * b).astype(BF)


def _swiglu_up(h, w1, w3, layer, tm=2048, tn=512):
    n, d = h.shape
    f = w1.shape[2]
    tm = min(tm, n)
    tn = _tile(f, tn)
    wspec = pl.BlockSpec((None, d, tn), lambda j, i: (layer, 0, j))
    return pl.pallas_call(
        functools.partial(_swiglu_up_kernel, rc=min(512, tm)),
        grid=(f // tn, n // tm),
        in_specs=[pl.BlockSpec((tm, d), lambda j, i: (i, 0)), wspec, wspec],
        out_specs=pl.BlockSpec((tm, tn), lambda j, i: (i, j)),
        out_shape=jax.ShapeDtypeStruct((n, f), BF),
        scratch_shapes=[pltpu.VMEM((d, tn), BF), pltpu.VMEM((d, tn), BF)],
        compiler_params=_params(("arbitrary", "arbitrary")),
        name="dense_swiglu_up",
    )(h, w1, w3)


def _down_residual_kernel(a_ref, w_ref, x_ref, g_ref, o_ref):
    y = jnp.dot(a_ref[...], w_ref[...], preferred_element_type=F32)
    o_ref[...] = x_ref[...] + g_ref[0] * y


def _down_residual(a, w, layer, x, gate, seq, tm=1024, tn=512):
    n, k = a.shape
    d = w.shape[2]
    tm = min(tm, seq)
    tn = min(tn, d)
    bpb = seq // tm
    return pl.pallas_call(
        _down_residual_kernel,
        grid=(d // tn, n // tm),
        in_specs=[pl.BlockSpec((tm, k), lambda j, i: (i, 0)),
                  pl.BlockSpec((None, k, tn), lambda j, i: (layer, 0, j)),
                  pl.BlockSpec((tm, tn), lambda j, i: (i, j)),
                  pl.BlockSpec((1, 1, tn), lambda j, i: (i // bpb, 0, j))],
        out_specs=pl.BlockSpec((tm, tn), lambda j, i: (i, j)),
        out_shape=jax.ShapeDtypeStruct((n, d), F32),
        compiler_params=_params(("arbitrary", "arbitrary")),
        name="dense_down_residual",
    )(a, w, x, gate)


def _row_copy(src, dst, sem, src_row, dst_row):
    return pltpu.make_async_copy(src.at[pl.ds(src_row, 1), :], dst.at[pl.ds(dst_row, 1), :], sem)


def _gather_kernel(tok_ref, base_ref, lim_ref, used_ref, h_hbm, o_ref, buf, sem, *, tm):
    i = pl.program_id(0)
    used = used_ref[0]

    def issue(blk, slot):
        base = base_ref[blk]
        lim = lim_ref[blk]

        def body(r, carry):
            tok = tok_ref[jnp.minimum(base + r, lim)]
            _row_copy(h_hbm, buf.at[slot], sem.at[slot], tok, r).start()
            return carry

        lax.fori_loop(0, tm, body, 0, unroll=DMA_UNROLL)

    @pl.when(i == 0)
    def _():
        issue(0, 0)

    @pl.when(i + 1 < used)
    def _():
        issue(i + 1, (i + 1) % 2)

    @pl.when(i < used)
    def _():
        slot = i % 2
        pltpu.make_async_copy(h_hbm.at[pl.ds(0, tm), :], buf.at[slot], sem.at[slot]).wait()
        o_ref[...] = buf[slot].astype(BF)

    @pl.when(i >= used)
    def _():
        o_ref[...] = jnp.zeros_like(o_ref)


def _moe_gather(plan, h, tm):
    rows = plan["nb"] * tm
    d = h.shape[1]
    return pl.pallas_call(
        functools.partial(_gather_kernel, tm=tm),
        grid_spec=pltpu.PrefetchScalarGridSpec(
            num_scalar_prefetch=4,
            grid=(plan["nb"],),
            in_specs=[pl.BlockSpec(memory_space=pl.ANY)],
            out_specs=pl.BlockSpec((tm, d), lambda i, *_: (i, 0)),
            scratch_shapes=[pltpu.VMEM((2, tm, d), F32), pltpu.SemaphoreType.DMA((2,))]),
        out_shape=jax.ShapeDtypeStruct((rows, d), BF),
        compiler_params=_params(("arbitrary",)),
        name="moe_gather",
    )(plan["tok_sorted"], plan["base"], plan["lim"], plan["used"], h)


def _moe_up_kernel(be_ref, first_ref, nexte_ref, lastrun_ref, used_ref, x_ref, w1_hbm, w3_hbm, o_ref,
                   stage, wb, sem, *, layer, tn, rc):
    j = pl.program_id(0)
    i = pl.program_id(1)

    def fetch(e, jj):
        cols = pl.ds(pl.multiple_of(jj * tn, LANES), tn)
        return [pltpu.make_async_copy(w.at[layer, e, :, cols], stage.at[k], sem.at[k])
                for k, w in enumerate((w1_hbm, w3_hbm))]

    @pl.when((j == 0) & (i == 0))
    def _():
        for cp in fetch(be_ref[0], 0):
            cp.start()

    @pl.when(first_ref[i] == 1)
    def _():
        for k, cp in enumerate(fetch(be_ref[i], j)):
            cp.wait()
            wb[k] = stage[k].astype(BF)
        nj = j + lastrun_ref[i]

        @pl.when(nj < pl.num_programs(0))
        def _():
            for cp in fetch(nexte_ref[i], nj):
                cp.start()

    @pl.when(i < used_ref[0])
    def _():
        for r0 in range(0, x_ref.shape[0], rc):
            x = x_ref[r0:r0 + rc, :]
            a = jnp.dot(x, wb[0], preferred_element_type=F32)
            b = jnp.dot(x, wb[1], preferred_element_type=F32)
            o_ref[r0:r0 + rc, :] = (a * jax.nn.sigmoid(a) * b).astype(BF)

    @pl.when(i >= used_ref[0])
    def _():
        o_ref[...] = jnp.zeros_like(o_ref)


def _moe_up(plan, xs, w1, w3, layer, tm, tn=1792):
    rows, d = xs.shape
    f = w1.shape[3]
    tn = _tile(f, tn)
    last = lambda i, used: jnp.minimum(i, used[0] - 1)
    return pl.pallas_call(
        functools.partial(_moe_up_kernel, layer=layer, tn=tn, rc=min(256, tm)),
        grid_spec=pltpu.PrefetchScalarGridSpec(
            num_scalar_prefetch=5,
            grid=(f // tn, rows // tm),
            in_specs=[pl.BlockSpec((tm, d), lambda j, i, be, first, nexte, lastrun, used: (last(i, used), 0)),
                      pl.BlockSpec(memory_space=pl.ANY),
                      pl.BlockSpec(memory_space=pl.ANY)],
            out_specs=pl.BlockSpec((tm, tn), lambda j, i, *_: (i, j)),
            scratch_shapes=[pltpu.VMEM((2, d, tn), F32), pltpu.VMEM((2, d, tn), BF),
                            pltpu.SemaphoreType.DMA((2,))]),
        out_shape=jax.ShapeDtypeStruct((rows, f), BF),
        compiler_params=_params(("arbitrary", "arbitrary")),
        name="moe_swiglu_up",
    )(plan["blk_expert"], plan["first"], plan["nexte"], plan["lastrun"], plan["used"], xs, w1, w3)


def _moe_down_kernel(be_ref, first_ref, used_ref, h_ref, w_ref, o_ref, wb):
    i = pl.program_id(1)

    @pl.when(first_ref[i] == 1)
    def _():
        wb[...] = w_ref[...].astype(BF)

    @pl.when(i < used_ref[0])
    def _():
        o_ref[...] = jnp.dot(h_ref[...], wb[...], preferred_element_type=F32)

    @pl.when(i >= used_ref[0])
    def _():
        o_ref[...] = jnp.zeros_like(o_ref)


def _moe_down(plan, hs, w2, layer, tm, tn=512):
    rows, f = hs.shape
    d = w2.shape[3]
    tn = min(tn, d)
    last = lambda i, used: jnp.minimum(i, used[0] - 1)
    return pl.pallas_call(
        _moe_down_kernel,
        grid_spec=pltpu.PrefetchScalarGridSpec(
            num_scalar_prefetch=3,
            grid=(d // tn, rows // tm),
            in_specs=[pl.BlockSpec((tm, f), lambda j, i, be, first, used: (last(i, used), 0)),
                      pl.BlockSpec((None, None, f, tn),
                                   lambda j, i, be, first, used: (layer, be[last(i, used)], 0, j))],
            out_specs=pl.BlockSpec((tm, tn), lambda j, i, be, first, used: (i, j)),
            scratch_shapes=[pltpu.VMEM((f, tn), BF)]),
        out_shape=jax.ShapeDtypeStruct((rows, d), F32),
        compiler_params=_params(("arbitrary", "arbitrary")),
        name="moe_down",
    )(plan["blk_expert"], plan["first"], plan["used"], hs, w2)


def _combine_kernel(pos_ref, ys_hbm, x_ref, tg_ref, g2_ref, fin_ref, o_ref, buf, sem, *, tm, final):
    i = pl.program_id(0)
    steps = pl.num_programs(0)

    def issue(tile, slot):
        def body(r, carry):
            for k in range(TOP_K):
                src = pos_ref[(tile * tm + r) * TOP_K + k]
                _row_copy(ys_hbm, buf.at[slot], sem.at[slot], src, k * tm + r).start()
            return carry

        lax.fori_loop(0, tm, body, 0, unroll=DMA_UNROLL)

    @pl.when(i == 0)
    def _():
        issue(0, 0)

    @pl.when(i + 1 < steps)
    def _():
        issue(i + 1, (i + 1) % 2)

    slot = i % 2
    pltpu.make_async_copy(ys_hbm.at[pl.ds(0, TOP_K * tm), :], buf.at[slot], sem.at[slot]).wait()
    y = tg_ref[:, 0:1] * buf[slot, 0:tm, :]
    for k in range(1, TOP_K):
        y = y + tg_ref[:, k:k + 1] * buf[slot, k * tm:(k + 1) * tm, :]
    res = x_ref[...] + g2_ref[0] * y
    o_ref[...] = _rms(res, fin_ref[...]) if final else res


def _moe_combine(pos, ys, x, tg, g2, final_g, final, seq, tm=TM_COMBINE):
    n, d = x.shape
    tm = min(tm, seq)
    bpb = seq // tm
    return pl.pallas_call(
        functools.partial(_combine_kernel, tm=tm, final=final),
        grid_spec=pltpu.PrefetchScalarGridSpec(
            num_scalar_prefetch=1,
            grid=(n // tm,),
            in_specs=[pl.BlockSpec(memory_space=pl.ANY),
                      pl.BlockSpec((tm, d), lambda i, pos: (i, 0)),
                      pl.BlockSpec((tm, LANES), lambda i, pos: (i, 0)),
                      pl.BlockSpec((1, 1, d), lambda i, pos: (i // bpb, 0, 0)),
                      _resident((1, d), lambda i, pos: (0, 0))],
            out_specs=pl.BlockSpec((tm, d), lambda i, pos: (i, 0)),
            scratch_shapes=[pltpu.VMEM((2, TOP_K * tm, d), F32), pltpu.SemaphoreType.DMA((2,))]),
        out_shape=jax.ShapeDtypeStruct((n, d), F32),
        compiler_params=_params(("arbitrary",)),
        name="moe_combine",
    )(pos, ys, x, tg, g2, final_g.reshape(1, d))


def _dispatch_plan(top_idx, tm):
    n = top_idx.shape[0]
    pairs = n * TOP_K
    e_flat = top_idx.reshape(-1)
    eids = jnp.arange(N_EXPERTS, dtype=jnp.int32)
    onehot = (e_flat[:, None] == eids[None, :]).astype(jnp.int32)
    csum = jnp.cumsum(onehot, axis=0)
    counts = csum[-1]
    padded = (counts + tm - 1) // tm * tm
    pend = jnp.cumsum(padded)
    pstart = pend - padded
    start = jnp.cumsum(counts) - counts
    pos = jnp.sum(onehot * (csum - 1 + pstart[None, :]), axis=1).astype(jnp.int32)
    order = jnp.argsort(e_flat, stable=True)
    tok_sorted = (order // TOP_K).astype(jnp.int32)
    nb = pairs // tm + N_EXPERTS
    used = (pend[-1] // tm).astype(jnp.int32)
    blk_start = jnp.arange(nb, dtype=jnp.int32) * tm
    be = jnp.minimum(jnp.sum(blk_start[:, None] >= pend[None, :], axis=1), N_EXPERTS - 1)
    be_last = jnp.minimum(jnp.sum((used - 1) * tm >= pend), N_EXPERTS - 1)
    be = jnp.where(jnp.arange(nb) < used, be, be_last).astype(jnp.int32)
    sel = (be[:, None] == eids[None, :]).astype(jnp.int32)
    start_b = jnp.sum(sel * start[None, :], axis=1)
    base = (start_b + blk_start - jnp.sum(sel * pstart[None, :], axis=1)).astype(jnp.int32)
    lim = (start_b + jnp.sum(sel * counts[None, :], axis=1) - 1).astype(jnp.int32)
    first = jnp.concatenate([jnp.ones((1,), jnp.int32), (be[1:] != be[:-1]).astype(jnp.int32)])
    later = be[None, :] > be[:, None]
    lastrun = jnp.logical_not(jnp.any(later, axis=1))
    nexte = jnp.where(lastrun, be[0], jnp.min(jnp.where(later, be[None, :], N_EXPERTS), axis=1)).astype(jnp.int32)
    return dict(pos=pos, tok_sorted=tok_sorted, blk_expert=be, first=first, nexte=nexte,
                lastrun=lastrun.astype(jnp.int32), base=base, lim=lim, used=used.reshape(1), nb=nb)


def _moe(h2, ti, tg, x, g2, w1, w3, w2, layer, final_g, final, seq):
    plan = _dispatch_plan(ti[:, :TOP_K], TM_MOE)
    xs = _moe_gather(plan, h2, TM_MOE)
    hs = _moe_up(plan, xs, w1, w3, layer, TM_MOE)
    ys = _moe_down(plan, hs, w2, layer, TM_MOE)
    return _moe_combine(plan["pos"], ys, x, tg, g2, final_g, final, seq)


def _final_norm_kernel(x_ref, g_ref, o_ref):
    o_ref[...] = _rms(x_ref[...], g_ref[...])


def _final_norm(x, g, tm=512):
    n, d = x.shape
    tm = min(tm, n)
    return pl.pallas_call(
        _final_norm_kernel,
        grid=(n // tm,),
        in_specs=[pl.BlockSpec((tm, d), lambda i: (i, 0)), _resident((1, d), lambda i: (0, 0))],
        out_specs=pl.BlockSpec((tm, d), lambda i: (i, 0)),
        out_shape=jax.ShapeDtypeStruct((n, d), F32),
        compiler_params=_params(("arbitrary",)),
        name="final_norm",
    )(x, g.reshape(1, d))


def kernel(x, c, ada_w, ada_b, norm_mix_g, norm_ffn_g, final_g, ev_w_in, ev_w_out, gm_ws, gm_b, gm_norm_g, cv_w, cv_b, cv_ln_g, cv_ln_b, ffn_w1, ffn_w3, ffn_w2, od_w_in, od_w_out, pool_w, pool_scale, sc_w, sc_b, router_w, moe_w1, moe_w3, moe_w2):
    bsz, seq, d = x.shape
    depth = ada_b.shape[0]
    xf = x.reshape(bsz * seq, d)
    mod = _ada(c, ada_w, ada_b)
    ev_w_in_b, ev_w_out_b = ev_w_in.astype(BF), ev_w_out.astype(BF)
    od_w_in_b, od_w_out_b = od_w_in.astype(BF), od_w_out.astype(BF)
    ffn_w2_b, pool_w_b = ffn_w2.astype(BF), pool_w.astype(BF)
    for l in range(depth):
        m = mod[l].reshape(bsz, 6, 1, d)
        sh1, sc1, g1, sh2, sc2, g2 = [m[:, k] for k in range(6)]
        i = l // 2
        last = l == depth - 1
        if l % 2 == 0:
            z = _norm_mm(xf, norm_mix_g[l], sc1, sh1, ev_w_in_b, i, seq)
            xf, h2 = _even_mix(z, xf, g1, ev_w_out_b, i, gm_ws[i], gm_b[i], gm_norm_g[i],
                               cv_w[i], cv_b[i], cv_ln_g[i], cv_ln_b[i], norm_ffn_g[l], sc2, sh2, seq)
            hs = _swiglu_up(h2, ffn_w1, ffn_w3, i)
            xf = _down_residual(hs, ffn_w2_b, i, xf, g2, seq)
            if last:
                xf = _final_norm(xf, final_g)
        else:
            z = _norm_mm(xf, norm_mix_g[l], sc1, sh1, od_w_in_b, i, seq)
            xf, h2, ti, tg = _odd_mix(z, xf, g1, od_w_out_b, i, pool_w_b[i], pool_scale[i],
                                      sc_w[i], sc_b[i], norm_ffn_g[l], sc2, sh2, router_w[i], seq)
            xf = _moe(h2, ti, tg, xf, g2, moe_w1, moe_w3, moe_w2, i, final_g, last, seq)
    return xf.reshape(bsz, seq, d)
```

```python
import functools

import jax
import jax.numpy as jnp
from jax import lax
from jax.experimental import pallas as pl
from jax.experimental.pallas import tpu as pltpu

BF = jnp.bfloat16
F32 = jnp.float32
EPS = 1e-6

CHUNK = 64
GMLP_BLOCK = 128
N_GROUPS_A = 4
CONV_B = 31
CONV_D = 3
POOL_WINDOWS = (2, 4, 8, 16)
N_EXPERTS = 8
TOP_K = 2

LANES = 128
SUBLANES = 8
HALO = 32
VMEM_LIMIT = 60 * 1024 * 1024

TM_MIX = 256
TM_IN = 512
TM_MOE = 512
TM_COMBINE = 256
DMA_UNROLL = 8
CONV_ROWS = 64


def _params(sem):
    return pltpu.CompilerParams(dimension_semantics=sem, vmem_limit_bytes=VMEM_LIMIT)


def _resident(shape, index_map):
    return pl.BlockSpec(shape, index_map, pipeline_mode=pl.Buffered(1))


def _tile(n, want):
    t = min(want, n)
    while n % t:
        t -= LANES
    return t


def _rms(x, g):
    return x * lax.rsqrt(jnp.mean(x * x, axis=-1, keepdims=True) + EPS) * g


def _ada_kernel(c_ref, w_ref, b_ref, o_ref):
    c = c_ref[...]
    s = c * jax.nn.sigmoid(c)
    base = jnp.dot(s, w_ref[...], preferred_element_type=F32, precision=lax.Precision.HIGHEST)
    o_ref[...] = base[None, :, :] + b_ref[...][:, None, :]


def _ada(c, ada_w, ada_b, tn=1024):
    bsz, d = c.shape
    depth, n6 = ada_b.shape
    tn = _tile(n6, tn)
    return pl.pallas_call(
        _ada_kernel,
        grid=(n6 // tn,),
        in_specs=[pl.BlockSpec((bsz, d), lambda j: (0, 0)),
                  pl.BlockSpec((d, tn), lambda j: (0, j)),
                  pl.BlockSpec((depth, tn), lambda j: (0, j))],
        out_specs=pl.BlockSpec((depth, bsz, tn), lambda j: (0, 0, j)),
        out_shape=jax.ShapeDtypeStruct((depth, bsz, n6), F32),
        compiler_params=_params(("arbitrary",)),
        name="ada_mod",
    )(c, ada_w, ada_b)


def _norm_mm_kernel(x_ref, g_ref, sc_ref, sh_ref, w_ref, z_ref, *, tn):
    h = _rms(x_ref[...], g_ref[...]) * (1.0 + sc_ref[0]) + sh_ref[0]
    hb = h.astype(BF)
    nout = z_ref.shape[1]
    for n0 in range(0, nout, tn):
        z_ref[:, n0:n0 + tn] = jnp.dot(hb, w_ref[:, n0:n0 + tn], preferred_element_type=F32)


def _norm_mm(x, g, sc, sh, w, layer, seq, tm=TM_IN, tn=512):
    n, d = x.shape
    nout = w.shape[2]
    tm = min(tm, seq)
    bpb = seq // tm
    return pl.pallas_call(
        functools.partial(_norm_mm_kernel, tn=min(tn, nout)),
        grid=(n // tm,),
        in_specs=[pl.BlockSpec((tm, d), lambda i: (i, 0)),
                  _resident((1, d), lambda i: (0, 0)),
                  pl.BlockSpec((1, 1, d), lambda i: (i // bpb, 0, 0)),
                  pl.BlockSpec((1, 1, d), lambda i: (i // bpb, 0, 0)),
                  _resident((None, d, nout), lambda i: (layer, 0, 0))],
        out_specs=pl.BlockSpec((tm, nout), lambda i: (i, 0)),
        out_shape=jax.ShapeDtypeStruct((n, nout), F32),
        compiler_params=_params(("arbitrary",)),
        name="norm_in_proj",
    )(x, g.reshape(1, d), sc, sh, w)


def _out_proj_residual(ycat, x_ref, g1_ref, wout_ref, xo_ref, tn=512):
    d = xo_ref.shape[1]
    yc = ycat[...]
    for n0 in range(0, d, tn):
        y = jnp.dot(yc, wout_ref[:, n0:n0 + tn], preferred_element_type=F32)
        xo_ref[:, n0:n0 + tn] = x_ref[:, n0:n0 + tn] + g1_ref[0][:, n0:n0 + tn] * y


def _ffn_input(xo_ref, fg_ref, sc2_ref, sh2_ref):
    return _rms(xo_ref[...], fg_ref[...]) * (1.0 + sc2_ref[0]) + sh2_ref[0]


def _dwconv_lanes(gbuf, cvw_ref, cvb_ref, conv, cb, tm):
    lanes = slice(cb * LANES, (cb + 1) * LANES)
    off = HALO - (CONV_B - 1)
    rc = min(CONV_ROWS, tm)
    win = rc + HALO + SUBLANES
    wk = [cvw_ref[k:k + 1, lanes] for k in range(CONV_B)]
    bias = cvb_ref[:, lanes]
    for r0 in range(0, tm, rc):
        col = gbuf[r0:r0 + win, lanes]
        acc = None
        for r in range(SUBLANES):
            s = col if r == 0 else pltpu.roll(col, win - r, axis=0)
            for k in range(CONV_B):
                if (off + k) % SUBLANES != r:
                    continue
                q = (off + k) // SUBLANES
                t = wk[k] * s[SUBLANES * q:SUBLANES * q + rc, :]
                acc = t if acc is None else acc + t
        conv[r0:r0 + rc, lanes] = acc + bias


def _even_mix_kernel(z_ref, halo_ref, x_ref, g1_ref, wout_ref, ws_ref, gmb_ref, gng_ref,
                     cvw_ref, cvb_ref, lng_ref, lnb_ref, fg_ref, sc2_ref, sh2_ref,
                     xo_ref, h2_ref, gbuf, conv, ycat, *, tm, da, db, bpb):
    i = pl.program_id(0)
    dg = da // N_GROUPS_A

    r = lax.broadcasted_iota(jnp.int32, (GMLP_BLOCK, GMLP_BLOCK), 0) // CHUNK
    c = lax.broadcasted_iota(jnp.int32, (GMLP_BLOCK, GMLP_BLOCK), 1) // CHUNK
    causal = c <= r
    ws = [jnp.where(causal, ws_ref[g], 0.0).astype(BF) for g in range(N_GROUPS_A)]
    for n in range(tm // GMLP_BLOCK):
        rows = slice(n * GMLP_BLOCK, (n + 1) * GMLP_BLOCK)
        v = jax.nn.gelu(z_ref[rows, da:2 * da], approximate=True)
        vb = _rms(v, gng_ref[...]).astype(BF)
        for g in range(N_GROUPS_A):
            cols = slice(g * dg, (g + 1) * dg)
            sp = jnp.dot(ws[g], vb[:, cols], preferred_element_type=F32) + gmb_ref[:, g:g + 1]
            u = jax.nn.gelu(z_ref[rows, cols], approximate=True)
            ycat[rows, cols] = (u * sp).astype(BF)

    a = z_ref[:, 2 * da:2 * da + db]
    gate = z_ref[:, 2 * da + db:]
    gbuf[HALO:HALO + tm, :] = a * jax.nn.sigmoid(gate)
    hglu = halo_ref[:, 0:db] * jax.nn.sigmoid(halo_ref[:, db:2 * db])
    gbuf[0:HALO, :] = jnp.where(i % bpb == 0, 0.0, hglu)

    gbuf[HALO + tm:, :] = jnp.zeros((SUBLANES, db), F32)
    for cb in range(db // LANES):
        _dwconv_lanes(gbuf, cvw_ref, cvb_ref, conv, cb, tm)

    cv = conv[...]
    mu = jnp.mean(cv, axis=-1, keepdims=True)
    xc = cv - mu
    ln = xc * lax.rsqrt(jnp.mean(xc * xc, axis=-1, keepdims=True) + EPS) * lng_ref[...] + lnb_ref[...]
    ycat[:, da:da + db] = (ln * jax.nn.sigmoid(ln)).astype(BF)

    _out_proj_residual(ycat, x_ref, g1_ref, wout_ref, xo_ref)
    h2_ref[...] = _ffn_input(xo_ref, fg_ref, sc2_ref, sh2_ref).astype(h2_ref.dtype)


def _even_mix(z, x, g1, wout, layer, gm_ws, gm_b, gm_norm_g, cv_w, cv_b, cv_ln_g, cv_ln_b,
              ffn_g, sc2, sh2, seq, tm=TM_MIX):
    n, d = x.shape
    da = gm_norm_g.shape[0]
    db = cv_b.shape[0]
    tm = min(tm, seq)
    bpb = seq // tm
    hb = tm // HALO
    vec = lambda k: _resident((1, k), lambda i: (0, 0))
    per_batch = pl.BlockSpec((1, 1, d), lambda i: (i // bpb, 0, 0))
    return pl.pallas_call(
        functools.partial(_even_mix_kernel, tm=tm, da=da, db=db, bpb=bpb),
        grid=(n // tm,),
        in_specs=[pl.BlockSpec((tm, 2 * da + 2 * db), lambda i: (i, 0)),
                  pl.BlockSpec((HALO, 2 * db), lambda i: (jnp.maximum(i * hb - 1, 0), da // db)),
                  pl.BlockSpec((tm, d), lambda i: (i, 0)),
                  per_batch,
                  _resident((None, da + db, d), lambda i: (layer, 0, 0)),
                  _resident((N_GROUPS_A, GMLP_BLOCK, GMLP_BLOCK), lambda i: (0, 0, 0)),
                  _resident((GMLP_BLOCK, N_GROUPS_A), lambda i: (0, 0)),
                  vec(da),
                  _resident((CONV_B, db), lambda i: (0, 0)),
                  vec(db), vec(db), vec(db), vec(d),
                  per_batch, per_batch],
        out_specs=[pl.BlockSpec((tm, d), lambda i: (i, 0)),
                   pl.BlockSpec((tm, d), lambda i: (i, 0))],
        out_shape=[jax.ShapeDtypeStruct((n, d), F32),
                   jax.ShapeDtypeStruct((n, d), BF)],
        scratch_shapes=[pltpu.VMEM((HALO + tm + SUBLANES, db), F32),
                        pltpu.VMEM((tm, db), F32),
                        pltpu.VMEM((tm, da + db), BF)],
        compiler_params=_params(("arbitrary",)),
        name="even_mixer_out_proj",
    )(z, z, x, g1, wout, gm_ws, gm_b.T, gm_norm_g.reshape(1, da), cv_w, cv_b.reshape(1, db),
      cv_ln_g.reshape(1, db), cv_ln_b.reshape(1, db), ffn_g.reshape(1, d), sc2, sh2)


def _odd_mix_kernel(z_ref, halo_ref, x_ref, g1_ref, wout_ref, pw_ref, ps_ref, scw_ref, scb_ref,
                    fg_ref, sc2_ref, sh2_ref, rw_ref,
                    xo_ref, h2_ref, ti_ref, tg_ref, pbuf, qbuf, ycat, *, tm, dc, dd, bpb):
    i = pl.program_id(0)
    first = i % bpb == 0
    dgc = dc // len(POOL_WINDOWS)

    pbuf[HALO:HALO + tm, :] = z_ref[:, 0:dc]
    pbuf[0:HALO, :] = jnp.where(first, 0.0, halo_ref[:, 0:dc])
    pos = (i % bpb) * tm + lax.broadcasted_iota(jnp.int32, (tm, 1), 0)
    for g, w in enumerate(POOL_WINDOWS):
        cols = slice(g * dgc, (g + 1) * dgc)
        tok = pbuf[HALO:HALO + tm, cols]
        acc = tok
        for dlt in range(1, w):
            acc = acc + pbuf[HALO - dlt:HALO - dlt + tm, cols]
        cnt = jnp.minimum(pos + 1, w).astype(F32)
        diff = (acc / cnt - tok).astype(BF)
        mixed = jnp.dot(diff, pw_ref[g], preferred_element_type=F32) * ps_ref[:, cols]
        ycat[:, cols] = mixed.astype(BF)

    qbuf[HALO:HALO + tm, :] = z_ref[:, dc + dd:dc + 2 * dd] * z_ref[:, dc + 2 * dd:]
    hp = halo_ref[:, dc + dd:dc + 2 * dd] * halo_ref[:, dc + 2 * dd:]
    qbuf[0:HALO, :] = jnp.where(first, 0.0, hp)
    off = HALO - (CONV_D - 1)
    y = scw_ref[0:1, :] * qbuf[off:off + tm, :]
    for k in range(1, CONV_D):
        y = y + scw_ref[k:k + 1, :] * qbuf[off + k:off + k + tm, :]
    y = y + scb_ref[...]
    ycat[:, dc:dc + dd] = (z_ref[:, dc:dc + dd] * y).astype(BF)

    _out_proj_residual(ycat, x_ref, g1_ref, wout_ref, xo_ref)
    h2 = _ffn_input(xo_ref, fg_ref, sc2_ref, sh2_ref)
    h2_ref[...] = h2

    hh = h2.astype(BF)
    hl = (h2 - hh.astype(F32)).astype(BF)
    p = jnp.dot(hh, rw_ref[...], preferred_element_type=F32)
    q = jnp.dot(hl, rw_ref[:, 0:LANES], preferred_element_type=F32)
    logits = p[:, 0:LANES] + p[:, LANES:] + q

    lane = lax.broadcasted_iota(jnp.int32, logits.shape, 1)
    lane_f = lane.astype(F32)
    neg = jnp.float32(-jnp.inf)
    l1 = jnp.where(lane < N_EXPERTS, logits, neg)
    m1 = jnp.max(l1, axis=-1, keepdims=True)
    i1 = jnp.min(jnp.where(l1 == m1, lane_f, float(LANES)), axis=-1, keepdims=True)
    l2 = jnp.where(lane_f == i1, neg, l1)
    m2 = jnp.max(l2, axis=-1, keepdims=True)
    i2 = jnp.min(jnp.where(l2 == m2, lane_f, float(LANES)), axis=-1, keepdims=True)
    e = jnp.exp(m2 - m1)
    ga = 1.0 / (1.0 + e)
    gb = e / (1.0 + e)
    ti_ref[...] = jnp.where(lane == 0, i1, jnp.where(lane == 1, i2, 0.0)).astype(jnp.int32)
    tg_ref[...] = jnp.where(lane == 0, ga, jnp.where(lane == 1, gb, 0.0))


def _odd_mix(z, x, g1, wout, layer, pool_w, pool_scale, sc_w, sc_b, ffn_g, sc2, sh2, router_w, seq, tm=TM_MIX):
    n, d = x.shape
    dc = pool_scale.shape[0]
    dd = sc_b.shape[0]
    ng = len(POOL_WINDOWS)
    dgc = dc // ng
    tm = min(tm, seq)
    bpb = seq // tm
    hb = tm // HALO
    rw_hi = router_w.astype(BF)
    rw_lo = (router_w - rw_hi.astype(F32)).astype(BF)
    rw = jnp.zeros((d, 2 * LANES), BF).at[:, :N_EXPERTS].set(rw_hi).at[:, LANES:LANES + N_EXPERTS].set(rw_lo)
    vec = lambda k: _resident((1, k), lambda i: (0, 0))
    per_batch = pl.BlockSpec((1, 1, d), lambda i: (i // bpb, 0, 0))
    return pl.pallas_call(
        functools.partial(_odd_mix_kernel, tm=tm, dc=dc, dd=dd, bpb=bpb),
        grid=(n // tm,),
        in_specs=[pl.BlockSpec((tm, dc + 3 * dd), lambda i: (i, 0)),
                  pl.BlockSpec((HALO, dc + 3 * dd), lambda i: (jnp.maximum(i * hb - 1, 0), 0)),
                  pl.BlockSpec((tm, d), lambda i: (i, 0)),
                  per_batch,
                  _resident((None, dc + dd, d), lambda i: (layer, 0, 0)),
                  _resident((ng, dgc, dgc), lambda i: (0, 0, 0)),
                  vec(dc),
                  _resident((CONV_D, dd), lambda i: (0, 0)),
                  vec(dd), vec(d),
                  per_batch, per_batch,
                  _resident((d, 2 * LANES), lambda i: (0, 0))],
        out_specs=[pl.BlockSpec((tm, d), lambda i: (i, 0)),
                   pl.BlockSpec((tm, d), lambda i: (i, 0)),
                   pl.BlockSpec((tm, LANES), lambda i: (i, 0)),
                   pl.BlockSpec((tm, LANES), lambda i: (i, 0))],
        out_shape=[jax.ShapeDtypeStruct((n, d), F32),
                   jax.ShapeDtypeStruct((n, d), F32),
                   jax.ShapeDtypeStruct((n, LANES), jnp.int32),
                   jax.ShapeDtypeStruct((n, LANES), F32)],
        scratch_shapes=[pltpu.VMEM((HALO + tm, dc), F32),
                        pltpu.VMEM((HALO + tm, dd), F32),
                        pltpu.VMEM((tm, dc + dd), BF)],
        compiler_params=_params(("arbitrary",)),
        name="odd_mixer_out_proj_router",
    )(z, z, x, g1, wout, pool_w, pool_scale.reshape(1, dc), sc_w, sc_b.reshape(1, dd),
      ffn_g.reshape(1, d), sc2, sh2, rw)


def _swiglu_up_kernel(h_ref, w1_ref, w3_ref, o_ref, w1b, w3b, *, rc):
    @pl.when(pl.program_id(1) == 0)
    def _():
        w1b[...] = w1_ref[...].astype(BF)
        w3b[...] = w3_ref[...].astype(BF)

    for r0 in range(0, h_ref.shape[0], rc):
        h = h_ref[r0:r0 + rc, :]
        a = jnp.dot(h, w1b[...], preferred_element_type=F32)
        b = jnp.dot(h, w3b[...], preferred_element_type=F32)
        o_ref[r0:r0 + rc, :] = (a * jax.nn.sigmoid(a) * b).astype(BF)


def _swiglu_up(h, w1, w3, layer, tm=2048, tn=512):
    n, d = h.shape
    f = w1.shape[2]
    tm = min(tm, n)
    tn = _tile(f, tn)
    wspec = pl.BlockSpec((None, d, tn), lambda j, i: (layer, 0, j))
    return pl.pallas_call(
        functools.partial(_swiglu_up_kernel, rc=min(512, tm)),
        grid=(f // tn, n // tm),
        in_specs=[pl.BlockSpec((tm, d), lambda j, i: (i, 0)), wspec, wspec],
        out_specs=pl.BlockSpec((tm, tn), lambda j, i: (i, j)),
        out_shape=jax.ShapeDtypeStruct((n, f), BF),
        scratch_shapes=[pltpu.VMEM((d, tn), BF), pltpu.VMEM((d, tn), BF)],
        compiler_params=_params(("arbitrary", "arbitrary")),
        name="dense_swiglu_up",
    )(h, w1, w3)


def _down_residual_kernel(a_ref, w_ref, x_ref, g_ref, o_ref):
    y = jnp.dot(a_ref[...], w_ref[...], preferred_element_type=F32)
    o_ref[...] = x_ref[...] + g_ref[0] * y


def _down_residual(a, w, layer, x, gate, seq, tm=1024, tn=512):
    n, k = a.shape
    d = w.shape[2]
    tm = min(tm, seq)
    tn = min(tn, d)
    bpb = seq // tm
    return pl.pallas_call(
        _down_residual_kernel,
        grid=(d // tn, n // tm),
        in_specs=[pl.BlockSpec((tm, k), lambda j, i: (i, 0)),
                  pl.BlockSpec((None, k, tn), lambda j, i: (layer, 0, j)),
                  pl.BlockSpec((tm, tn), lambda j, i: (i, j)),
                  pl.BlockSpec((1, 1, tn), lambda j, i: (i // bpb, 0, j))],
        out_specs=pl.BlockSpec((tm, tn), lambda j, i: (i, j)),
        out_shape=jax.ShapeDtypeStruct((n, d), F32),
        compiler_params=_params(("arbitrary", "arbitrary")),
        name="dense_down_residual",
    )(a, w, x, gate)


def _row_copy(src, dst, sem, src_row, dst_row):
    return pltpu.make_async_copy(src.at[pl.ds(src_row, 1), :], dst.at[pl.ds(dst_row, 1), :], sem)


def _gather_kernel(tok_ref, base_ref, lim_ref, used_ref, h_hbm, w_ref, o_ref, wo_ref, buf, sem, *, tm, nconv):
    i = pl.program_id(0)
    used = used_ref[0]

    @pl.when(i < nconv)
    def _():
        wo_ref[...] = w_ref[...].astype(BF)

    def issue(blk, slot):
        base = base_ref[blk]
        lim = lim_ref[blk]

        def body(r, carry):
            tok = tok_ref[jnp.minimum(base + r, lim)]
            _row_copy(h_hbm, buf.at[slot], sem.at[slot], tok, r).start()
            return carry

        lax.fori_loop(0, tm, body, 0, unroll=DMA_UNROLL)

    @pl.when(i == 0)
    def _():
        issue(0, 0)

    @pl.when(i + 1 < used)
    def _():
        issue(i + 1, (i + 1) % 2)

    @pl.when(i < used)
    def _():
        slot = i % 2
        pltpu.make_async_copy(h_hbm.at[pl.ds(0, tm), :], buf.at[slot], sem.at[slot]).wait()
        o_ref[...] = buf[slot].astype(BF)

    @pl.when(i >= used)
    def _():
        o_ref[...] = jnp.zeros_like(o_ref)


def _moe_gather(plan, h, w2, layer, tm):
    rows = plan["nb"] * tm
    d = h.shape[1]
    n_layers, n_exp, f, dout = w2.shape
    nconv = plan["nb"] - N_EXPERTS
    assert (n_exp * f) % nconv == 0
    slab = n_exp * f // nconv
    wslab = lambda i, *_: (jnp.minimum(i, nconv - 1), 0)
    xs, w2b = pl.pallas_call(
        functools.partial(_gather_kernel, tm=tm, nconv=nconv),
        grid_spec=pltpu.PrefetchScalarGridSpec(
            num_scalar_prefetch=4,
            grid=(plan["nb"],),
            in_specs=[pl.BlockSpec(memory_space=pl.ANY),
                      pl.BlockSpec((None, slab, dout), lambda i, *_: (layer,) + wslab(i))],
            out_specs=[pl.BlockSpec((tm, d), lambda i, *_: (i, 0)),
                       pl.BlockSpec((slab, dout), wslab)],
            scratch_shapes=[pltpu.VMEM((2, tm, d), F32), pltpu.SemaphoreType.DMA((2,))]),
        out_shape=[jax.ShapeDtypeStruct((rows, d), BF),
                   jax.ShapeDtypeStruct((n_exp * f, dout), BF)],
        compiler_params=_params(("arbitrary",)),
        name="moe_gather",
    )(plan["tok_sorted"], plan["base"], plan["lim"], plan["used"], h, w2.reshape(n_layers, n_exp * f, dout))
    return xs, w2b.reshape(n_exp, f, dout)


def _moe_up_kernel(be_ref, first_ref, nexte_ref, lastrun_ref, used_ref, x_ref, w1_hbm, w3_hbm, o_ref,
                   stage, wb, sem, *, layer, tn, rc):
    j = pl.program_id(0)
    i = pl.program_id(1)

    def fetch(e, jj):
        cols = pl.ds(pl.multiple_of(jj * tn, LANES), tn)
        return [pltpu.make_async_copy(w.at[layer, e, :, cols], stage.at[k], sem.at[k])
                for k, w in enumerate((w1_hbm, w3_hbm))]

    @pl.when((j == 0) & (i == 0))
    def _():
        for cp in fetch(be_ref[0], 0):
            cp.start()

    @pl.when(first_ref[i] == 1)
    def _():
        for k, cp in enumerate(fetch(be_ref[i], j)):
            cp.wait()
            wb[k] = stage[k].astype(BF)
        nj = j + lastrun_ref[i]

        @pl.when(nj < pl.num_programs(0))
        def _():
            for cp in fetch(nexte_ref[i], nj):
                cp.start()

    @pl.when(i < used_ref[0])
    def _():
        for r0 in range(0, x_ref.shape[0], rc):
            x = x_ref[r0:r0 + rc, :]
            a = jnp.dot(x, wb[0], preferred_element_type=F32)
            b = jnp.dot(x, wb[1], preferred_element_type=F32)
            o_ref[r0:r0 + rc, :] = (a * jax.nn.sigmoid(a) * b).astype(BF)

    @pl.when(i >= used_ref[0])
    def _():
        o_ref[...] = jnp.zeros_like(o_ref)


def _moe_up(plan, xs, w1, w3, layer, tm, tn=1792):
    rows, d = xs.shape
    f = w1.shape[3]
    tn = _tile(f, tn)
    last = lambda i, used: jnp.minimum(i, used[0] - 1)
    return pl.pallas_call(
        functools.partial(_moe_up_kernel, layer=layer, tn=tn, rc=min(256, tm)),
        grid_spec=pltpu.PrefetchScalarGridSpec(
            num_scalar_prefetch=5,
            grid=(f // tn, rows // tm),
            in_specs=[pl.BlockSpec((tm, d), lambda j, i, be, first, nexte, lastrun, used: (last(i, used), 0)),
                      pl.BlockSpec(memory_space=pl.ANY),
                      pl.BlockSpec(memory_space=pl.ANY)],
            out_specs=pl.BlockSpec((tm, tn), lambda j, i, *_: (i, j)),
            scratch_shapes=[pltpu.VMEM((2, d, tn), F32), pltpu.VMEM((2, d, tn), BF),
                            pltpu.SemaphoreType.DMA((2,))]),
        out_shape=jax.ShapeDtypeStruct((rows, f), BF),
        compiler_params=_params(("arbitrary", "arbitrary")),
        name="moe_swiglu_up",
    )(plan["blk_expert"], plan["first"], plan["nexte"], plan["lastrun"], plan["used"], xs, w1, w3)


def _moe_down_kernel(be_ref, used_ref, h_ref, w_ref, o_ref):
    i = pl.program_id(1)

    @pl.when(i < used_ref[0])
    def _():
        o_ref[...] = jnp.dot(h_ref[...], w_ref[...], preferred_element_type=F32)

    @pl.when(i >= used_ref[0])
    def _():
        o_ref[...] = jnp.zeros_like(o_ref)


def _moe_down(plan, hs, w2b, tm, tn=1024):
    rows, f = hs.shape
    d = w2b.shape[2]
    tn = min(tn, d)
    last = lambda i, used: jnp.minimum(i, used[0] - 1)
    return pl.pallas_call(
        _moe_down_kernel,
        grid_spec=pltpu.PrefetchScalarGridSpec(
            num_scalar_prefetch=2,
            grid=(d // tn, rows // tm),
            in_specs=[pl.BlockSpec((tm, f), lambda j, i, be, used: (last(i, used), 0)),
                      pl.BlockSpec((None, f, tn), lambda j, i, be, used: (be[last(i, used)], 0, j))],
            out_specs=pl.BlockSpec((tm, tn), lambda j, i, be, used: (i, j))),
        out_shape=jax.ShapeDtypeStruct((rows, d), F32),
        compiler_params=_params(("arbitrary", "arbitrary")),
        name="moe_down",
    )(plan["blk_expert"], plan["used"], hs, w2b)


def _combine_kernel(pos_ref, ys_hbm, x_ref, tg_ref, g2_ref, fin_ref, o_ref, buf, sem, *, tm, final):
    i = pl.program_id(0)
    steps = pl.num_programs(0)

    def issue(tile, slot):
        def body(r, carry):
            for k in range(TOP_K):
                src = pos_ref[(tile * tm + r) * TOP_K + k]
                _row_copy(ys_hbm, buf.at[slot], sem.at[slot], src, k * tm + r).start()
            return carry

        lax.fori_loop(0, tm, body, 0, unroll=DMA_UNROLL)

    @pl.when(i == 0)
    def _():
        issue(0, 0)

    @pl.when(i + 1 < steps)
    def _():
        issue(i + 1, (i + 1) % 2)

    slot = i % 2
    pltpu.make_async_copy(ys_hbm.at[pl.ds(0, TOP_K * tm), :], buf.at[slot], sem.at[slot]).wait()
    y = tg_ref[:, 0:1] * buf[slot, 0:tm, :]
    for k in range(1, TOP_K):
        y = y + tg_ref[:, k:k + 1] * buf[slot, k * tm:(k + 1) * tm, :]
    res = x_ref[...] + g2_ref[0] * y
    o_ref[...] = _rms(res, fin_ref[...]) if final else res


def _moe_combine(pos, ys, x, tg, g2, final_g, final, seq, tm=TM_COMBINE):
    n, d = x.shape
    tm = min(tm, seq)
    bpb = seq // tm
    return pl.pallas_call(
        functools.partial(_combine_kernel, tm=tm, final=final),
        grid_spec=pltpu.PrefetchScalarGridSpec(
            num_scalar_prefetch=1,
            grid=(n // tm,),
            in_specs=[pl.BlockSpec(memory_space=pl.ANY),
                      pl.BlockSpec((tm, d), lambda i, pos: (i, 0)),
                      pl.BlockSpec((tm, LANES), lambda i, pos: (i, 0)),
                      pl.BlockSpec((1, 1, d), lambda i, pos: (i // bpb, 0, 0)),
                      _resident((1, d), lambda i, pos: (0, 0))],
            out_specs=pl.BlockSpec((tm, d), lambda i, pos: (i, 0)),
            scratch_shapes=[pltpu.VMEM((2, TOP_K * tm, d), F32), pltpu.SemaphoreType.DMA((2,))]),
        out_shape=jax.ShapeDtypeStruct((n, d), F32),
        compiler_params=_params(("arbitrary",)),
        name="moe_combine",
    )(pos, ys, x, tg, g2, final_g.reshape(1, d))


def _dispatch_plan(top_idx, tm):
    n = top_idx.shape[0]
    pairs = n * TOP_K
    e_flat = top_idx.reshape(-1)
    eids = jnp.arange(N_EXPERTS, dtype=jnp.int32)
    onehot = (e_flat[:, None] == eids[None, :]).astype(jnp.int32)
    csum = jnp.cumsum(onehot, axis=0)
    counts = csum[-1]
    padded = (counts + tm - 1) // tm * tm
    pend = jnp.cumsum(padded)
    pstart = pend - padded
    start = jnp.cumsum(counts) - counts
    pos = jnp.sum(onehot * (csum - 1 + pstart[None, :]), axis=1).astype(jnp.int32)
    order = jnp.argsort(e_flat, stable=True)
    tok_sorted = (order // TOP_K).astype(jnp.int32)
    nb = pairs // tm + N_EXPERTS
    used = (pend[-1] // tm).astype(jnp.int32)
    blk_start = jnp.arange(nb, dtype=jnp.int32) * tm
    be = jnp.minimum(jnp.sum(blk_start[:, None] >= pend[None, :], axis=1), N_EXPERTS - 1)
    be_last = jnp.minimum(jnp.sum((used - 1) * tm >= pend), N_EXPERTS - 1)
    be = jnp.where(jnp.arange(nb) < used, be, be_last).astype(jnp.int32)
    sel = (be[:, None] == eids[None, :]).astype(jnp.int32)
    start_b = jnp.sum(sel * start[None, :], axis=1)
    base = (start_b + blk_start - jnp.sum(sel * pstart[None, :], axis=1)).astype(jnp.int32)
    lim = (start_b + jnp.sum(sel * counts[None, :], axis=1) - 1).astype(jnp.int32)
    first = jnp.concatenate([jnp.ones((1,), jnp.int32), (be[1:] != be[:-1]).astype(jnp.int32)])
    later = be[None, :] > be[:, None]
    lastrun = jnp.logical_not(jnp.any(later, axis=1))
    nexte = jnp.where(lastrun, be[0], jnp.min(jnp.where(later, be[None, :], N_EXPERTS), axis=1)).astype(jnp.int32)
    return dict(pos=pos, tok_sorted=tok_sorted, blk_expert=be, first=first, nexte=nexte,
                lastrun=lastrun.astype(jnp.int32), base=base, lim=lim, used=used.reshape(1), nb=nb)


def _moe(h2, ti, tg, x, g2, w1, w3, w2, layer, final_g, final, seq):
    plan = _dispatch_plan(ti[:, :TOP_K], TM_MOE)
    xs, w2b = _moe_gather(plan, h2, w2, layer, TM_MOE)
    hs = _moe_up(plan, xs, w1, w3, layer, TM_MOE)
    ys = _moe_down(plan, hs, w2b, TM_MOE)
    return _moe_combine(plan["pos"], ys, x, tg, g2, final_g, final, seq)


def _final_norm_kernel(x_ref, g_ref, o_ref):
    o_ref[...] = _rms(x_ref[...], g_ref[...])


def _final_norm(x, g, tm=512):
    n, d = x.shape
    tm = min(tm, n)
    return pl.pallas_call(
        _final_norm_kernel,
        grid=(n // tm,),
        in_specs=[pl.BlockSpec((tm, d), lambda i: (i, 0)), _resident((1, d), lambda i: (0, 0))],
        out_specs=pl.BlockSpec((tm, d), lambda i: (i, 0)),
        out_shape=jax.ShapeDtypeStruct((n, d), F32),
        compiler_params=_params(("arbitrary",)),
        name="final_norm",
    )(x, g.reshape(1, d))


def kernel(x, c, ada_w, ada_b, norm_mix_g, norm_ffn_g, final_g, ev_w_in, ev_w_out, gm_ws, gm_b, gm_norm_g, cv_w, cv_b, cv_ln_g, cv_ln_b, ffn_w1, ffn_w3, ffn_w2, od_w_in, od_w_out, pool_w, pool_scale, sc_w, sc_b, router_w, moe_w1, moe_w3, moe_w2):
    bsz, seq, d = x.shape
    depth = ada_b.shape[0]
    xf = x.reshape(bsz * seq, d)
    mod = _ada(c, ada_w, ada_b)
    ev_w_in_b, ev_w_out_b = ev_w_in.astype(BF), ev_w_out.astype(BF)
    od_w_in_b, od_w_out_b = od_w_in.astype(BF), od_w_out.astype(BF)
    ffn_w2_b, pool_w_b = ffn_w2.astype(BF), pool_w.astype(BF)
    for l in range(depth):
        m = mod[l].reshape(bsz, 6, 1, d)
        sh1, sc1, g1, sh2, sc2, g2 = [m[:, k] for k in range(6)]
        i = l // 2
        last = l == depth - 1
        if l % 2 == 0:
            z = _norm_mm(xf, norm_mix_g[l], sc1, sh1, ev_w_in_b, i, seq)
            xf, h2 = _even_mix(z, xf, g1, ev_w_out_b, i, gm_ws[i], gm_b[i], gm_norm_g[i],
                               cv_w[i], cv_b[i], cv_ln_g[i], cv_ln_b[i], norm_ffn_g[l], sc2, sh2, seq)
            hs = _swiglu_up(h2, ffn_w1, ffn_w3, i)
            xf = _down_residual(hs, ffn_w2_b, i, xf, g2, seq)
            if last:
                xf = _final_norm(xf, final_g)
        else:
            z = _norm_mm(xf, norm_mix_g[l], sc1, sh1, od_w_in_b, i, seq)
            xf, h2, ti, tg = _odd_mix(z, xf, g1, od_w_out_b, i, pool_w_b[i], pool_scale[i],
                                      sc_w[i], sc_b[i], norm_ffn_g[l], sc2, sh2, router_w[i], seq)
            xf = _moe(h2, ti, tg, xf, g2, moe_w1, moe_w3, moe_w2, i, final_g, last, seq)
    return xf.reshape(bsz, seq, d)
```
